```python
import jax, jax.numpy as jnp
from jax import lax
import numpy as np

D_MODEL = 4096
BATCH = 1
SEQ = 8192
DEPTH = 2

N_HEADS = 32
N_KV_HEADS = 8
HEAD_DIM = 128
Q_GROUP = N_HEADS // N_KV_HEADS
ATTN_WIDTH = N_HEADS * HEAD_DIM
KV_WIDTH = N_KV_HEADS * HEAD_DIM
AXIS_ROT_DIM = HEAD_DIM // 2
ROPE_THETA = 10000.0
Q_BLOCK = 128
GRID_W = 64

POOL_WINDOWS = (2, 4, 8, 16)
N_POOL_GROUPS = 4
POOL_WIDTH = 2048
POOL_GROUP_IN = POOL_WIDTH // N_POOL_GROUPS
POOL_GROUP_OUT = D_MODEL // N_POOL_GROUPS

IN_WIDTH = ATTN_WIDTH + 2 * KV_WIDTH + POOL_WIDTH + 2 * D_MODEL

D_FF = -(-8 * D_MODEL // (3 * 256)) * 256

D_PLE = 256

EPS = 1e-6

kernel_name = "hybrid_gqa_axialrope_multipool_swiglu_ple"


def rms_norm(x, g):
    xf = x.astype(jnp.float32)
    y = xf * lax.rsqrt(jnp.mean(xf * xf, axis=-1, keepdims=True) + EPS)
    return (y * g.astype(jnp.float32)).astype(x.dtype)


def axial_rope_tables(seq):
    rows = seq // GRID_W
    row = jnp.broadcast_to(jnp.arange(rows)[:, None], (rows, GRID_W)).reshape(seq)
    col = jnp.broadcast_to(jnp.arange(GRID_W)[None, :], (rows, GRID_W)).reshape(seq)
    inv_freq = ROPE_THETA ** (-jnp.arange(0, AXIS_ROT_DIM, 2, dtype=jnp.float32) / AXIS_ROT_DIM)
    pos = jnp.stack([row, col], axis=-1).astype(jnp.float32)
    ang = pos[:, :, None] * inv_freq[None, None, :]
    return jnp.cos(ang), jnp.sin(ang)


def apply_axial_rope(x, cos, sin):
    b, s, h, d = x.shape
    xf = x.astype(jnp.float32).reshape(b, s, h, 2, 2, AXIS_ROT_DIM // 2)
    x1, x2 = xf[..., 0, :], xf[..., 1, :]
    c, sn = cos[None, :, None], sin[None, :, None]
    out = jnp.stack([x1 * c - x2 * sn, x2 * c + x1 * sn], axis=-2)
    return out.reshape(b, s, h, d).astype(x.dtype)


def block_attention(q, k, v):
    b, s = q.shape[:2]
    nb = s // Q_BLOCK
    qb = q.reshape(b, nb, Q_BLOCK, N_KV_HEADS, Q_GROUP, HEAD_DIM).transpose(1, 0, 2, 3, 4, 5)
    scale = HEAD_DIM ** -0.5

    def one_block(qblk):
        sc = jnp.einsum('bqkgd,bskd->bkgqs', qblk, k,
                        preferred_element_type=jnp.float32) * scale
        pr = jax.nn.softmax(sc, axis=-1).astype(v.dtype)
        return jnp.einsum('bkgqs,bskd->bqkgd', pr, v)

    out = lax.map(one_block, qb)
    return out.transpose(1, 0, 2, 3, 4, 5).reshape(b, s, ATTN_WIDTH)


def multiscale_pool(u, w_pool, pool_scale):
    b, s, _ = u.shape
    ug = u.astype(jnp.float32).reshape(b, s, N_POOL_GROUPS, POOL_GROUP_IN)
    csum = jnp.concatenate([jnp.zeros((b, 1, N_POOL_GROUPS, POOL_GROUP_IN), jnp.float32),
                            jnp.cumsum(ug, axis=1)], axis=1)
    t = jnp.arange(s)[:, None]
    half = jnp.array(POOL_WINDOWS, dtype=jnp.int32)[None, :] // 2
    lo = jnp.clip(t - half, 0, s)
    hi = jnp.clip(t + half, 0, s)
    grp = jnp.arange(N_POOL_GROUPS)[None, :]
    window_sum = csum[:, hi, grp, :] - csum[:, lo, grp, :]
    count = (hi - lo).astype(jnp.float32)[None, :, :, None]
    delta = (window_sum / count - ug).astype(u.dtype)
    y = jnp.einsum('bsgc,gcd->bsgd', delta, w_pool).reshape(b, s, D_MODEL)
    return y * pool_scale


def hybrid_layer(x, p_i, w_in, q_norm, k_norm, w_pool, pool_scale, w_out,
                 norm_mix_pre, norm_mix_post, norm_ffn_pre, norm_ffn_post,
                 w_ffn_gate, w_ffn_up, w_ffn_down, w_ple_in, w_ple_gate, norm_ple, cos, sin):
    b, s, _ = x.shape
    h = rms_norm(x, norm_mix_pre)
    z = h @ w_in
    o1 = ATTN_WIDTH
    o2 = o1 + KV_WIDTH
    o3 = o2 + KV_WIDTH
    o4 = o3 + POOL_WIDTH
    o5 = o4 + D_MODEL
    q = z[..., :o1].reshape(b, s, N_HEADS, HEAD_DIM)
    k = z[..., o1:o2].reshape(b, s, N_KV_HEADS, HEAD_DIM)
    v = z[..., o2:o3].reshape(b, s, N_KV_HEADS, HEAD_DIM)
    u = z[..., o3:o4]
    gate_attn = jax.nn.sigmoid(z[..., o4:o5])
    gate_pool = jax.nn.sigmoid(z[..., o5:])

    q = apply_axial_rope(rms_norm(q, q_norm), cos, sin)
    k = apply_axial_rope(rms_norm(k, k_norm), cos, sin)
    attn = block_attention(q, k, v)
    pool = multiscale_pool(u, w_pool, pool_scale)

    merged = gate_attn * attn + gate_pool * pool
    x = x + rms_norm(merged @ w_out, norm_mix_post)

    h = rms_norm(x, norm_ffn_pre)
    f = (jax.nn.silu(h @ w_ffn_gate) * (h @ w_ffn_up)) @ w_ffn_down
    x = x + rms_norm(f, norm_ffn_post)

    e = p_i @ w_ple_in
    g = jax.nn.sigmoid(x @ w_ple_gate)
    x = x + rms_norm(g * e, norm_ple)
    return x


def setup_inputs(seed: int = 0) -> dict:
    key = jax.random.key(seed)
    ks = jax.random.split(key, 20)
    f32 = jnp.float32

    def w(k, shape, fan_in):
        return jax.random.normal(k, shape, f32) * (fan_in ** -0.5)

    def gain(k, shape):
        return 1.0 + 0.1 * jax.random.normal(k, shape, f32)

    return {
        "x": jax.random.normal(ks[0], (BATCH, SEQ, D_MODEL), f32),
        "p": jax.random.normal(ks[1], (DEPTH, BATCH, SEQ, D_PLE), f32),
        "w_in": w(ks[2], (DEPTH, D_MODEL, IN_WIDTH), D_MODEL),
        "q_norm": gain(ks[3], (DEPTH, HEAD_DIM)),
        "k_norm": gain(ks[4], (DEPTH, HEAD_DIM)),
        "w_pool": w(ks[5], (DEPTH, N_POOL_GROUPS, POOL_GROUP_IN, POOL_GROUP_OUT), POOL_GROUP_IN),
        "pool_scale": gain(ks[6], (DEPTH, D_MODEL)),
        "w_out": w(ks[7], (DEPTH, D_MODEL, D_MODEL), D_MODEL),
        "norm_mix_pre": gain(ks[8], (DEPTH, D_MODEL)),
        "norm_mix_post": gain(ks[9], (DEPTH, D_MODEL)),
        "norm_ffn_pre": gain(ks[10], (DEPTH, D_MODEL)),
        "norm_ffn_post": gain(ks[11], (DEPTH, D_MODEL)),
        "w_ffn_gate": w(ks[12], (DEPTH, D_MODEL, D_FF), D_MODEL),
        "w_ffn_up": w(ks[13], (DEPTH, D_MODEL, D_FF), D_MODEL),
        "w_ffn_down": w(ks[14], (DEPTH, D_FF, D_MODEL), D_FF),
        "w_ple_in": w(ks[15], (DEPTH, D_PLE, D_MODEL), D_PLE),
        "w_ple_gate": w(ks[16], (DEPTH, D_MODEL, D_MODEL), D_MODEL),
        "norm_ple": gain(ks[17], (DEPTH, D_MODEL)),
    }


def reference(x, p, w_in, q_norm, k_norm, w_pool, pool_scale, w_out,
              norm_mix_pre, norm_mix_post, norm_ffn_pre, norm_ffn_post,
              w_ffn_gate, w_ffn_up, w_ffn_down, w_ple_in, w_ple_gate, norm_ple):
    seq = x.shape[1]
    cos, sin = axial_rope_tables(seq)
    h = x
    for i in range(DEPTH):
        h = hybrid_layer(h, p[i], w_in[i], q_norm[i], k_norm[i], w_pool[i], pool_scale[i], w_out[i],
                         norm_mix_pre[i], norm_mix_post[i], norm_ffn_pre[i], norm_ffn_post[i],
                         w_ffn_gate[i], w_ffn_up[i], w_ffn_down[i],
                         w_ple_in[i], w_ple_gate[i], norm_ple[i], cos, sin)
    return h
```

```python
import functools

import jax
import jax.numpy as jnp
from jax import lax
from jax.experimental import pallas as pl
from jax.experimental.pallas import tpu as pltpu

D_MODEL = 4096
SEQ = 8192
DEPTH = 2
N_HEADS = 32
N_KV_HEADS = 8
HEAD_DIM = 128
Q_GROUP = N_HEADS // N_KV_HEADS
ATTN_WIDTH = N_HEADS * HEAD_DIM
KV_WIDTH = N_KV_HEADS * HEAD_DIM
AXIS_ROT_DIM = HEAD_DIM // 2
ROT_HALF = AXIS_ROT_DIM // 2
ROPE_THETA = 10000.0
GRID_W = 64
POOL_WINDOWS = (2, 4, 8, 16)
N_POOL_GROUPS = 4
POOL_WIDTH = 2048
POOL_GROUP_IN = POOL_WIDTH // N_POOL_GROUPS
POOL_GROUP_OUT = D_MODEL // N_POOL_GROUPS
D_FF = 11008
D_PLE = 256
EPS = 1e-6

COL_Q = 0
COL_K = COL_Q + ATTN_WIDTH
COL_V = COL_K + KV_WIDTH
COL_U = COL_V + KV_WIDTH
COL_GATES = COL_U + POOL_WIDTH

V7X_VMEM_BYTES = 64 * 1024 * 1024
V7X_SUBLANES = 8
POOL_HALO = 8
VMEM_TEMP_BYTES = 12 * 1024 * 1024

F32 = jnp.float32
BF16 = jnp.bfloat16


def _params(semantics, block_bytes):
    limit = min(int(block_bytes) + VMEM_TEMP_BYTES, V7X_VMEM_BYTES - 4 * 1024 * 1024)
    return pltpu.CompilerParams(dimension_semantics=semantics, vmem_limit_bytes=limit)


def _rms(x, gain):
    ms = jnp.mean(x * x, axis=-1, keepdims=True)
    return x * lax.rsqrt(ms + EPS) * gain


def _norm_cast_kernel(x_ref, g_ref, o_ref):
    o_ref[...] = _rms(x_ref[...], g_ref[...]).astype(o_ref.dtype)


def _norm_cast(x, gain, tm=256):
    s, d = x.shape
    return pl.pallas_call(
        _norm_cast_kernel,
        grid=(s // tm,),
        in_specs=[pl.BlockSpec((tm, d), lambda i: (i, 0)),
                  pl.BlockSpec((1, d), lambda i: (0, 0))],
        out_specs=pl.BlockSpec((tm, d), lambda i: (i, 0)),
        out_shape=jax.ShapeDtypeStruct((s, d), BF16),
        compiler_params=_params(("arbitrary",), 2 * tm * d * 6),
        name="norm_cast",
    )(x, gain.reshape(1, d))


def _resnorm_kernel(x_ref, f_ref, gpost_ref, *rest, mode):
    xn = x_ref[...] + _rms(f_ref[...], gpost_ref[...])
    if mode == "norm":
        gnext_ref, xo_ref, ho_ref = rest
        ho_ref[...] = _rms(xn, gnext_ref[...]).astype(ho_ref.dtype)
    elif mode == "cast":
        xo_ref, ho_ref = rest
        ho_ref[...] = xn.astype(ho_ref.dtype)
    else:
        (xo_ref,) = rest
    xo_ref[...] = xn


def _resnorm(x, f, gpost, gnext=None, mode="none", tm=256):
    s, d = x.shape
    row = pl.BlockSpec((tm, d), lambda i: (i, 0))
    vec = pl.BlockSpec((1, d), lambda i: (0, 0))
    in_specs = [row, row, vec]
    args = [x, f, gpost.reshape(1, d)]
    if mode == "norm":
        in_specs.append(vec)
        args.append(gnext.reshape(1, d))
    out_shape = [jax.ShapeDtypeStruct((s, d), F32)]
    out_specs = [row]
    if mode != "none":
        out_shape.append(jax.ShapeDtypeStruct((s, d), BF16))
        out_specs.append(row)
    out = pl.pallas_call(
        functools.partial(_resnorm_kernel, mode=mode),
        grid=(s // tm,),
        in_specs=in_specs,
        out_specs=out_specs,
        out_shape=out_shape,
        compiler_params=_params(("arbitrary",), 2 * tm * d * 14),
        name="resnorm_" + mode,
    )(*args)
    return out if mode != "none" else (out[0], None)


def _proj_kernel(a_ref, w_ref, *rest, epilogue):
    acc = jnp.dot(a_ref[...], w_ref[...], preferred_element_type=F32)
    if epilogue == "none":
        (o_ref,) = rest
        o_ref[...] = acc.astype(o_ref.dtype)
    elif epilogue == "sigmoid":
        (o_ref,) = rest
        o_ref[...] = jax.nn.sigmoid(acc).astype(o_ref.dtype)
    elif epilogue == "ple":
        p_ref, wp_ref, o_ref = rest
        e = jnp.dot(p_ref[...], wp_ref[...], preferred_element_type=F32)
        o_ref[...] = (jax.nn.sigmoid(acc) * e).astype(o_ref.dtype)
    elif epilogue == "qk":
        g_ref, cos_ref, sin_ref, o_ref, acc_ref = rest
        acc_ref[...] = acc
        lane = lax.broadcasted_iota(jnp.int32, (1, HEAD_DIM), 1)
        first_half = (lane & ROT_HALF) == 0
        gain = g_ref[...]
        cos = cos_ref[...]
        sin = sin_ref[...]
        for h in range(acc_ref.shape[1] // HEAD_DIM):
            sl = slice(h * HEAD_DIM, (h + 1) * HEAD_DIM)
            y = _rms(acc_ref[:, sl], gain)
            partner = jnp.where(first_half,
                                pltpu.roll(y, HEAD_DIM - ROT_HALF, 1),
                                pltpu.roll(y, ROT_HALF, 1))
            o_ref[:, sl] = (y * cos + partner * sin).astype(o_ref.dtype)
    else:
        raise ValueError(epilogue)


def _proj(a, w, col0, ncols, out_dtype, epilogue="none", extra=(), tm=1024, tn=1024):
    m, k = a.shape
    tn = min(tn, ncols)
    assert m % tm == 0 and ncols % tn == 0 and col0 % tn == 0 and w.shape[0] == k
    cb = col0 // tn
    in_specs = [pl.BlockSpec((tm, k), lambda i, j: (i, 0)),
                pl.BlockSpec((k, tn), lambda i, j: (0, cb + j))]
    args = [a, w]
    scratch = []
    extra_bytes = 0
    if epilogue == "qk":
        gain, cos, sin = extra
        in_specs += [pl.BlockSpec((1, HEAD_DIM), lambda i, j: (0, 0)),
                     pl.BlockSpec((tm, HEAD_DIM), lambda i, j: (i, 0)),
                     pl.BlockSpec((tm, HEAD_DIM), lambda i, j: (i, 0))]
        args += [gain.reshape(1, HEAD_DIM), cos, sin]
        scratch = [pltpu.VMEM((tm, tn), F32)]
        extra_bytes = tm * tn * 4 + 4 * tm * HEAD_DIM * 4
    elif epilogue == "ple":
        p, wp = extra
        kp = p.shape[1]
        in_specs += [pl.BlockSpec((tm, kp), lambda i, j: (i, 0)),
                     pl.BlockSpec((kp, tn), lambda i, j: (0, j))]
        args += [p, wp]
        extra_bytes = 2 * (tm * kp + kp * tn) * 2
    out_bytes = jnp.dtype(out_dtype).itemsize
    block_bytes = 2 * (tm * k * 2 + k * tn * 2 + tm * tn * out_bytes) + extra_bytes
    return pl.pallas_call(
        functools.partial(_proj_kernel, epilogue=epilogue),
        grid=(m // tm, ncols // tn),
        in_specs=in_specs,
        out_specs=pl.BlockSpec((tm, tn), lambda i, j: (i, j)),
        out_shape=jax.ShapeDtypeStruct((m, ncols), out_dtype),
        scratch_shapes=scratch,
        compiler_params=_params(("arbitrary", "arbitrary"), block_bytes),
        name="proj_" + epilogue,
    )(*args)


def _flash_kernel(q_ref, k_ref, v_ref, o_ref, m_ref, l_ref, acc_ref):
    kj = pl.program_id(2)

    @pl.when(kj == 0)
    def _():
        m_ref[...] = jnp.full(m_ref.shape, -jnp.inf, F32)
        l_ref[...] = jnp.zeros(l_ref.shape, F32)
        acc_ref[...] = jnp.zeros(acc_ref.shape, F32)

    k = k_ref[...]
    v = v_ref[...]
    for h in range(Q_GROUP):
        q = q_ref[:, h * HEAD_DIM:(h + 1) * HEAD_DIM]
        s = lax.dot_general(q, k, (((1,), (1,)), ((), ())), preferred_element_type=F32)
        m_prev = m_ref[h]
        m_new = jnp.maximum(m_prev, jnp.max(s, axis=1, keepdims=True))
        alpha = jnp.exp(m_prev - m_new)
        p = jnp.exp(s - m_new)
        l_ref[h] = alpha * l_ref[h] + jnp.sum(p, axis=1, keepdims=True)
        acc_ref[h] = alpha * acc_ref[h] + jnp.dot(p.astype(BF16), v, preferred_element_type=F32)
        m_ref[h] = m_new

    @pl.when(kj == pl.num_programs(2) - 1)
    def _():
        for h in range(Q_GROUP):
            o_ref[:, h * HEAD_DIM:(h + 1) * HEAD_DIM] = (acc_ref[h] / l_ref[h]).astype(o_ref.dtype)


def _attention(q, k, v, tq=512, tk=512):
    s = q.shape[0]
    gw = Q_GROUP * HEAD_DIM
    block_bytes = (2 * (tq * gw * 2 + 2 * tk * HEAD_DIM * 2 + tq * gw * 4)
                   + Q_GROUP * tq * (2 * 128 + HEAD_DIM) * 4 + 4 * tq * tk * 4)
    return pl.pallas_call(
        _flash_kernel,
        grid=(N_KV_HEADS, s // tq, s // tk),
        in_specs=[pl.BlockSpec((tq, gw), lambda g, i, j: (i, g)),
                  pl.BlockSpec((tk, HEAD_DIM), lambda g, i, j: (j, g)),
                  pl.BlockSpec((tk, HEAD_DIM), lambda g, i, j: (j, g))],
        out_specs=pl.BlockSpec((tq, gw), lambda g, i, j: (i, g)),
        out_shape=jax.ShapeDtypeStruct((s, ATTN_WIDTH), F32),
        scratch_shapes=[pltpu.VMEM((Q_GROUP, tq, 1), F32),
                        pltpu.VMEM((Q_GROUP, tq, 1), F32),
                        pltpu.VMEM((Q_GROUP, tq, HEAD_DIM), F32)],
        compiler_params=_params(("arbitrary", "arbitrary", "arbitrary"), block_bytes),
        name="flash_attention",
    )(q, k, v)


def _pool_merge_kernel(up_ref, uc_ref, un_ref, wp_ref, ps_ref, gates_ref, attn_ref, o_ref, ext_ref,
                       *, seq):
    i = pl.program_id(0)
    tm = uc_ref.shape[0]
    last = pl.num_programs(0) - 1
    ext_ref[0:POOL_HALO, :] = jnp.where(i > 0, up_ref[...], 0.0)
    ext_ref[POOL_HALO:POOL_HALO + tm, :] = uc_ref[...]
    ext_ref[POOL_HALO + tm:, :] = jnp.where(i < last, un_ref[...], 0.0)
    t = i * tm + lax.broadcasted_iota(jnp.int32, (tm, 1), 0)
    for g in range(N_POOL_GROUPS):
        half = POOL_WINDOWS[g] // 2
        cols = slice(g * POOL_GROUP_IN, (g + 1) * POOL_GROUP_IN)
        wsum = ext_ref[POOL_HALO - half:POOL_HALO - half + tm, cols]
        for d in range(-half + 1, half):
            wsum = wsum + ext_ref[POOL_HALO + d:POOL_HALO + d + tm, cols]
        count = (jnp.minimum(t + half, seq) - jnp.maximum(t - half, 0)).astype(F32)
        delta = wsum / count - uc_ref[:, cols]
        y = jnp.dot(delta.astype(BF16), wp_ref[g], preferred_element_type=F32)
        oc = slice(g * POOL_GROUP_OUT, (g + 1) * POOL_GROUP_OUT)
        pool = y * ps_ref[:, oc]
        ga = gates_ref[:, oc]
        gp = gates_ref[:, D_MODEL + g * POOL_GROUP_OUT:D_MODEL + (g + 1) * POOL_GROUP_OUT]
        o_ref[:, oc] = (ga * attn_ref[:, oc] + gp * pool).astype(o_ref.dtype)


def _pool_merge(u, w_pool, pool_scale, gates, attn, tm=256):
    s = u.shape[0]
    hb = tm // POOL_HALO
    nhb = s // POOL_HALO
    block_bytes = (2 * (tm * POOL_WIDTH * 4 + 2 * POOL_HALO * POOL_WIDTH * 4 + tm * 2 * D_MODEL * 4
                        + tm * D_MODEL * 4 + tm * D_MODEL * 2 + w_pool.size * 2)
                   + (tm + 2 * POOL_HALO) * POOL_WIDTH * 4)
    return pl.pallas_call(
        functools.partial(_pool_merge_kernel, seq=s),
        grid=(s // tm,),
        in_specs=[pl.BlockSpec((POOL_HALO, POOL_WIDTH), lambda i: (jnp.maximum(i * hb - 1, 0), 0)),
                  pl.BlockSpec((tm, POOL_WIDTH), lambda i: (i, 0)),
                  pl.BlockSpec((POOL_HALO, POOL_WIDTH), lambda i: (jnp.minimum((i + 1) * hb, nhb - 1), 0)),
                  pl.BlockSpec(w_pool.shape, lambda i: (0, 0, 0)),
                  pl.BlockSpec((1, D_MODEL), lambda i: (0, 0)),
                  pl.BlockSpec((tm, 2 * D_MODEL), lambda i: (i, 0)),
                  pl.BlockSpec((tm, D_MODEL), lambda i: (i, 0))],
        out_specs=pl.BlockSpec((tm, D_MODEL), lambda i: (i, 0)),
        out_shape=jax.ShapeDtypeStruct((s, D_MODEL), BF16),
        scratch_shapes=[pltpu.VMEM((tm + 2 * POOL_HALO, POOL_WIDTH), F32)],
        compiler_params=_params(("arbitrary",), block_bytes),
        name="pool_merge",
    )(u, u, u, w_pool, pool_scale.reshape(1, D_MODEL), gates, attn)


def _ffn_kernel(h_ref, wg_ref, wu_ref, wd_ref, o_ref):
    f = pl.program_id(1)
    h = h_ref[...]
    g = jnp.dot(h, wg_ref[...], preferred_element_type=F32)
    u = jnp.dot(h, wu_ref[...], preferred_element_type=F32)
    a = (g * jax.nn.sigmoid(g) * u).astype(BF16)
    part = jnp.dot(a, wd_ref[...], preferred_element_type=F32)

    @pl.when(f == 0)
    def _():
        o_ref[...] = part

    @pl.when(f > 0)
    def _():
        o_ref[...] += part


def _ffn(h, wg, wu, wd, tm=512, tf=256):
    s, d = h.shape
    dff = wg.shape[1]
    assert dff % tf == 0
    block_bytes = 2 * (tm * d * 2 + 3 * d * tf * 2 + tm * d * 4)
    return pl.pallas_call(
        _ffn_kernel,
        grid=(s // tm, dff // tf),
        in_specs=[pl.BlockSpec((tm, d), lambda i, f: (i, 0)),
                  pl.BlockSpec((d, tf), lambda i, f: (0, f)),
                  pl.BlockSpec((d, tf), lambda i, f: (0, f)),
                  pl.BlockSpec((tf, d), lambda i, f: (f, 0))],
        out_specs=pl.BlockSpec((tm, d), lambda i, f: (i, 0)),
        out_shape=jax.ShapeDtypeStruct((s, d), F32),
        compiler_params=_params(("arbitrary", "arbitrary"), block_bytes),
        name="swiglu_ffn",
    )(h, wg, wu, wd)


def _rope_tables(seq):
    rows = seq // GRID_W
    row = jnp.broadcast_to(jnp.arange(rows)[:, None], (rows, GRID_W)).reshape(seq)
    col = jnp.broadcast_to(jnp.arange(GRID_W)[None, :], (rows, GRID_W)).reshape(seq)
    inv_freq = ROPE_THETA ** (-jnp.arange(0, AXIS_ROT_DIM, 2, dtype=F32) / AXIS_ROT_DIM)
    pos = jnp.stack([row, col], axis=-1).astype(F32)
    ang = pos[:, :, None] * inv_freq[None, None, :]
    cos, sin = jnp.cos(ang), jnp.sin(ang)
    cos_t = jnp.concatenate([cos, cos], axis=-1).reshape(seq, HEAD_DIM)
    sin_t = jnp.concatenate([-sin, sin], axis=-1).reshape(seq, HEAD_DIM)
    return cos_t, sin_t


def kernel(x, p, w_in, q_norm, k_norm, w_pool, pool_scale, w_out, norm_mix_pre, norm_mix_post,
           norm_ffn_pre, norm_ffn_post, w_ffn_gate, w_ffn_up, w_ffn_down, w_ple_in, w_ple_gate,
           norm_ple):
    b, s, d = x.shape
    assert b == 1 and d == D_MODEL and s % GRID_W == 0
    cos_t, sin_t = _rope_tables(s)
    xs = x.reshape(s, d)
    h = _norm_cast(xs, norm_mix_pre[0])
    for i in range(DEPTH):
        w_in_b = w_in[i].astype(BF16)
        q = _proj(h, w_in_b, COL_Q, ATTN_WIDTH, BF16, "qk",
                  (q_norm[i] * (HEAD_DIM ** -0.5), cos_t, sin_t))
        k = _proj(h, w_in_b, COL_K, KV_WIDTH, BF16, "qk", (k_norm[i], cos_t, sin_t))
        v = _proj(h, w_in_b, COL_V, KV_WIDTH, BF16)
        u = _proj(h, w_in_b, COL_U, POOL_WIDTH, F32)
        gates = _proj(h, w_in_b, COL_GATES, 2 * D_MODEL, F32, "sigmoid")
        attn = _attention(q, k, v)
        merged = _pool_merge(u, w_pool[i].astype(BF16), pool_scale[i], gates, attn)
        o = _proj(merged, w_out[i].astype(BF16), 0, D_MODEL, F32)
        xs, h = _resnorm(xs, o, norm_mix_post[i], norm_ffn_pre[i], "norm")
        f = _ffn(h, w_ffn_gate[i].astype(BF16), w_ffn_up[i].astype(BF16), w_ffn_down[i].astype(BF16))
        xs, xb = _resnorm(xs, f, norm_ffn_post[i], None, "cast")
        ge = _proj(xb, w_ple_gate[i].astype(BF16), 0, D_MODEL, F32, "ple",
                   (p[i].reshape(s, D_PLE).astype(BF16), w_ple_in[i].astype(BF16)))
        if i + 1 < DEPTH:
            xs, h = _resnorm(xs, ge, norm_ple[i], norm_mix_pre[i + 1], "norm")
        else:
            xs, _ = _resnorm(xs, ge, norm_ple[i], None, "none")
    return xs.reshape(b, s, d)
```

```python
import functools

import jax
import jax.numpy as jnp
from jax import lax
from jax.experimental import pallas as pl
from jax.experimental.pallas import tpu as pltpu

D_MODEL = 4096
SEQ = 8192
DEPTH = 2
N_HEADS = 32
N_KV_HEADS = 8
HEAD_DIM = 128
Q_GROUP = N_HEADS // N_KV_HEADS
ATTN_WIDTH = N_HEADS * HEAD_DIM
KV_WIDTH = N_KV_HEADS * HEAD_DIM
AXIS_ROT_DIM = HEAD_DIM // 2
ROT_HALF = AXIS_ROT_DIM // 2
ROPE_THETA = 10000.0
GRID_W = 64
POOL_WINDOWS = (2, 4, 8, 16)
N_POOL_GROUPS = 4
POOL_WIDTH = 2048
POOL_GROUP_IN = POOL_WIDTH // N_POOL_GROUPS
POOL_GROUP_OUT = D_MODEL // N_POOL_GROUPS
D_FF = 11008
D_PLE = 256
EPS = 1e-6
LOG2_E = 1.4426950408889634

COL_Q = 0
COL_K = COL_Q + ATTN_WIDTH
COL_V = COL_K + KV_WIDTH
COL_U = COL_V + KV_WIDTH
COL_GATES = COL_U + POOL_WIDTH

V7X_VMEM_BYTES = 64 * 1024 * 1024
V7X_SUBLANES = 8
POOL_HALO = 8
SOFTMAX_ROWS = 32
VMEM_TEMP_BYTES = 12 * 1024 * 1024

F32 = jnp.float32
BF16 = jnp.bfloat16


def _params(semantics, block_bytes):
    limit = min(int(block_bytes) + VMEM_TEMP_BYTES, V7X_VMEM_BYTES - 4 * 1024 * 1024)
    return pltpu.CompilerParams(dimension_semantics=semantics, vmem_limit_bytes=limit)


def _rms(x, gain):
    ms = jnp.mean(x * x, axis=-1, keepdims=True)
    return x * lax.rsqrt(ms + EPS) * gain


def _norm_cast_kernel(x_ref, g_ref, o_ref):
    o_ref[...] = _rms(x_ref[...], g_ref[...]).astype(o_ref.dtype)


def _norm_cast(x, gain, tm=256):
    s, d = x.shape
    return pl.pallas_call(
        _norm_cast_kernel,
        grid=(s // tm,),
        in_specs=[pl.BlockSpec((tm, d), lambda i: (i, 0)),
                  pl.BlockSpec((1, d), lambda i: (0, 0))],
        out_specs=pl.BlockSpec((tm, d), lambda i: (i, 0)),
        out_shape=jax.ShapeDtypeStruct((s, d), BF16),
        compiler_params=_params(("arbitrary",), 2 * tm * d * 6),
        name="norm_cast",
    )(x, gain.reshape(1, d))


def _resnorm_kernel(x_ref, f_ref, gpost_ref, *rest, mode):
    xn = x_ref[...] + _rms(f_ref[...], gpost_ref[...])
    if mode == "norm":
        gnext_ref, xo_ref, ho_ref = rest
        ho_ref[...] = _rms(xn, gnext_ref[...]).astype(ho_ref.dtype)
    elif mode == "cast":
        xo_ref, ho_ref = rest
        ho_ref[...] = xn.astype(ho_ref.dtype)
    else:
        (xo_ref,) = rest
    xo_ref[...] = xn


def _resnorm(x, f, gpost, gnext=None, mode="none", tm=256):
    s, d = x.shape
    row = pl.BlockSpec((tm, d), lambda i: (i, 0))
    vec = pl.BlockSpec((1, d), lambda i: (0, 0))
    in_specs = [row, row, vec]
    args = [x, f, gpost.reshape(1, d)]
    if mode == "norm":
        in_specs.append(vec)
        args.append(gnext.reshape(1, d))
    out_shape = [jax.ShapeDtypeStruct((s, d), F32)]
    out_specs = [row]
    if mode != "none":
        out_shape.append(jax.ShapeDtypeStruct((s, d), BF16))
        out_specs.append(row)
    out = pl.pallas_call(
        functools.partial(_resnorm_kernel, mode=mode),
        grid=(s // tm,),
        in_specs=in_specs,
        out_specs=out_specs,
        out_shape=out_shape,
        compiler_params=_params(("arbitrary",), 2 * tm * d * 14),
        name="resnorm_" + mode,
    )(*args)
    return out if mode != "none" else (out[0], None)


def _proj_kernel(a_ref, w_ref, *rest, epilogue):
    acc = jnp.dot(a_ref[...], w_ref[...], preferred_element_type=F32)
    if epilogue == "none":
        (o_ref,) = rest
        o_ref[...] = acc.astype(o_ref.dtype)
    elif epilogue == "transpose":
        (o_ref,) = rest
        o_ref[...] = acc.T.astype(o_ref.dtype)
    elif epilogue == "sigmoid":
        (o_ref,) = rest
        o_ref[...] = jax.nn.sigmoid(acc).astype(o_ref.dtype)
    elif epilogue == "ple":
        p_ref, wp_ref, o_ref = rest
        e = jnp.dot(p_ref[...], wp_ref[...], preferred_element_type=F32)
        o_ref[...] = (jax.nn.sigmoid(acc) * e).astype(o_ref.dtype)
    elif epilogue == "qk":
        g_ref, cos_ref, sin_ref, o_ref, acc_ref = rest
        acc_ref[...] = acc
        lane = lax.broadcasted_iota(jnp.int32, (1, HEAD_DIM), 1)
        first_half = (lane & ROT_HALF) == 0
        gain = g_ref[...]
        cos = cos_ref[...]
        sin = sin_ref[...]
        for h in range(acc_ref.shape[1] // HEAD_DIM):
            sl = slice(h * HEAD_DIM, (h + 1) * HEAD_DIM)
            y = _rms(acc_ref[:, sl], gain)
            partner = jnp.where(first_half,
                                pltpu.roll(y, HEAD_DIM - ROT_HALF, 1),
                                pltpu.roll(y, ROT_HALF, 1))
            o_ref[:, sl] = (y * cos + partner * sin).astype(o_ref.dtype)
    else:
        raise ValueError(epilogue)


def _proj(a, w, col0, ncols, out_dtype, epilogue="none", extra=(), tm=1024, tn=1024):
    m, k = a.shape
    tn = min(tn, ncols)
    assert m % tm == 0 and ncols % tn == 0 and col0 % tn == 0 and w.shape[0] == k
    cb = col0 // tn
    in_specs = [pl.BlockSpec((tm, k), lambda i, j: (i, 0)),
                pl.BlockSpec((k, tn), lambda i, j: (0, cb + j))]
    args = [a, w]
    scratch = []
    extra_bytes = 0
    if epilogue == "qk":
        gain, cos, sin = extra
        in_specs += [pl.BlockSpec((1, HEAD_DIM), lambda i, j: (0, 0)),
                     pl.BlockSpec((tm, HEAD_DIM), lambda i, j: (i, 0)),
                     pl.BlockSpec((tm, HEAD_DIM), lambda i, j: (i, 0))]
        args += [gain.reshape(1, HEAD_DIM), cos, sin]
        scratch = [pltpu.VMEM((tm, tn), F32)]
        extra_bytes = tm * tn * 4 + 4 * tm * HEAD_DIM * 4
    elif epilogue == "ple":
        p, wp = extra
        kp = p.shape[1]
        in_specs += [pl.BlockSpec((tm, kp), lambda i, j: (i, 0)),
                     pl.BlockSpec((kp, tn), lambda i, j: (0, j))]
        args += [p, wp]
        extra_bytes = 2 * (tm * kp + kp * tn) * 2
    out_bytes = jnp.dtype(out_dtype).itemsize
    block_bytes = 2 * (tm * k * 2 + k * tn * 2 + tm * tn * out_bytes) + extra_bytes
    if epilogue == "transpose":
        out_spec = pl.BlockSpec((tn, tm), lambda i, j: (j, i))
        out_shape = jax.ShapeDtypeStruct((ncols, m), out_dtype)
    else:
        out_spec = pl.BlockSpec((tm, tn), lambda i, j: (i, j))
        out_shape = jax.ShapeDtypeStruct((m, ncols), out_dtype)
    return pl.pallas_call(
        functools.partial(_proj_kernel, epilogue=epilogue),
        grid=(m // tm, ncols // tn),
        in_specs=in_specs,
        out_specs=out_spec,
        out_shape=out_shape,
        scratch_shapes=scratch,
        compiler_params=_params(("arbitrary", "arbitrary"), block_bytes),
        name="proj_" + epilogue,
    )(*args)


def _flash_kernel(q_ref, k_ref, vt_ref, o_ref, st_ref, pt_ref, acc_ref, *, tk):
    tq = q_ref.shape[0]
    nchunks = k_ref.shape[0] // tk
    qs = [q_ref[:, h * HEAD_DIM:(h + 1) * HEAD_DIM] for h in range(Q_GROUP)]

    def scores(c):
        kc = k_ref[pl.ds(pl.multiple_of(c * tk, tk), tk), :]
        for h in range(Q_GROUP):
            st_ref[h] = lax.dot_general(kc, qs[h], (((1,), (1,)), ((), ())),
                                        preferred_element_type=F32)

    def softmax_update(ms, ls):
        out = []
        nblk = tk // SOFTMAX_ROWS
        for h in range(Q_GROUP):
            blocks = [st_ref.at[h, r * SOFTMAX_ROWS:(r + 1) * SOFTMAX_ROWS, :] for r in range(nblk)]
            part = blocks[0][...]
            for blk in blocks[1:]:
                part = jnp.maximum(part, blk[...])
            m_new = jnp.maximum(ms[h], jnp.max(part, axis=0, keepdims=True))
            alpha = jnp.exp2(ms[h] - m_new)
            lpart = None
            for r, blk in enumerate(blocks):
                pt = jnp.exp2(blk[...] - m_new)
                pt_ref[h, r * SOFTMAX_ROWS:(r + 1) * SOFTMAX_ROWS, :] = pt.astype(BF16)
                lpart = pt if lpart is None else lpart + pt
            l_new = alpha * ls[h] + jnp.sum(lpart, axis=0, keepdims=True)
            out.append((m_new, l_new, alpha))
        return tuple(zip(*out))

    def value_update(c, alphas):
        vc = vt_ref[:, pl.ds(pl.multiple_of(c * tk, tk), tk)]
        for h in range(Q_GROUP):
            acc_ref[h] = alphas[h] * acc_ref[h] + jnp.dot(vc, pt_ref[h], preferred_element_type=F32)

    ms = tuple(jnp.full((1, tq), -jnp.inf, F32) for _ in range(Q_GROUP))
    ls = tuple(jnp.zeros((1, tq), F32) for _ in range(Q_GROUP))
    acc_ref[...] = jnp.zeros(acc_ref.shape, F32)
    scores(0)
    ms, ls, alphas = softmax_update(ms, ls)

    def body(c, carry):
        ms, ls, alphas = carry
        scores(c)
        value_update(c - 1, alphas)
        return softmax_update(ms, ls)

    ms, ls, alphas = lax.fori_loop(1, nchunks, body, (ms, ls, alphas))
    value_update(nchunks - 1, alphas)
    for h in range(Q_GROUP):
        o_ref[:, h * HEAD_DIM:(h + 1) * HEAD_DIM] = (acc_ref[h] / ls[h]).T.astype(o_ref.dtype)


def _attention(q, k, vt, tq=512, tk=512):
    s = q.shape[0]
    gw = Q_GROUP * HEAD_DIM
    block_bytes = (2 * (tq * gw * 2 + 2 * s * HEAD_DIM * 2 + tq * gw * 4)
                   + Q_GROUP * (tk * tq * (4 + 2) + HEAD_DIM * tq * 4))
    return pl.pallas_call(
        functools.partial(_flash_kernel, tk=tk),
        grid=(N_KV_HEADS, s // tq),
        in_specs=[pl.BlockSpec((tq, gw), lambda g, i: (i, g)),
                  pl.BlockSpec((s, HEAD_DIM), lambda g, i: (0, g)),
                  pl.BlockSpec((HEAD_DIM, s), lambda g, i: (g, 0))],
        out_specs=pl.BlockSpec((tq, gw), lambda g, i: (i, g)),
        out_shape=jax.ShapeDtypeStruct((s, ATTN_WIDTH), F32),
        scratch_shapes=[pltpu.VMEM((Q_GROUP, tk, tq), F32),
                        pltpu.VMEM((Q_GROUP, tk, tq), BF16),
                        pltpu.VMEM((Q_GROUP, HEAD_DIM, tq), F32)],
        compiler_params=_params(("arbitrary", "arbitrary"), block_bytes),
        name="flash_attention",
    )(q, k, vt)


def _pool_merge_kernel(up_ref, uc_ref, un_ref, wp_ref, ps_ref, gates_ref, attn_ref, o_ref, ext_ref,
                       *, seq):
    i = pl.program_id(0)
    tm = uc_ref.shape[0]
    last = pl.num_programs(0) - 1
    ext_ref[0:POOL_HALO, :] = jnp.where(i > 0, up_ref[...], 0.0)
    ext_ref[POOL_HALO:POOL_HALO + tm, :] = uc_ref[...]
    ext_ref[POOL_HALO + tm:, :] = jnp.where(i < last, un_ref[...], 0.0)
    t = i * tm + lax.broadcasted_iota(jnp.int32, (tm, 1), 0)
    for g in range(N_POOL_GROUPS):
        half = POOL_WINDOWS[g] // 2
        cols = slice(g * POOL_GROUP_IN, (g + 1) * POOL_GROUP_IN)
        wsum = ext_ref[POOL_HALO - half:POOL_HALO - half + tm, cols]
        for d in range(-half + 1, half):
            wsum = wsum + ext_ref[POOL_HALO + d:POOL_HALO + d + tm, cols]
        count = (jnp.minimum(t + half, seq) - jnp.maximum(t - half, 0)).astype(F32)
        delta = wsum / count - uc_ref[:, cols]
        y = jnp.dot(delta.astype(BF16), wp_ref[g], preferred_element_type=F32)
        oc = slice(g * POOL_GROUP_OUT, (g + 1) * POOL_GROUP_OUT)
        pool = y * ps_ref[:, oc]
        ga = gates_ref[:, oc]
        gp = gates_ref[:, D_MODEL + g * POOL_GROUP_OUT:D_MODEL + (g + 1) * POOL_GROUP_OUT]
        o_ref[:, oc] = (ga * attn_ref[:, oc] + gp * pool).astype(o_ref.dtype)


def _pool_merge(u, w_pool, pool_scale, gates, attn, tm=256):
    s = u.shape[0]
    hb = tm // POOL_HALO
    nhb = s // POOL_HALO
    block_bytes = (2 * (tm * POOL_WIDTH * 4 + 2 * POOL_HALO * POOL_WIDTH * 4 + tm * 2 * D_MODEL * 4
                        + tm * D_MODEL * 4 + tm * D_MODEL * 2 + w_pool.size * 2)
                   + (tm + 2 * POOL_HALO) * POOL_WIDTH * 4)
    return pl.pallas_call(
        functools.partial(_pool_merge_kernel, seq=s),
        grid=(s // tm,),
        in_specs=[pl.BlockSpec((POOL_HALO, POOL_WIDTH), lambda i: (jnp.maximum(i * hb - 1, 0), 0)),
                  pl.BlockSpec((tm, POOL_WIDTH), lambda i: (i, 0)),
                  pl.BlockSpec((POOL_HALO, POOL_WIDTH), lambda i: (jnp.minimum((i + 1) * hb, nhb - 1), 0)),
                  pl.BlockSpec(w_pool.shape, lambda i: (0, 0, 0)),
                  pl.BlockSpec((1, D_MODEL), lambda i: (0, 0)),
                  pl.BlockSpec((tm, 2 * D_MODEL), lambda i: (i, 0)),
                  pl.BlockSpec((tm, D_MODEL), lambda i: (i, 0))],
        out_specs=pl.BlockSpec((tm, D_MODEL), lambda i: (i, 0)),
        out_shape=jax.ShapeDtypeStruct((s, D_MODEL), BF16),
        scratch_shapes=[pltpu.VMEM((tm + 2 * POOL_HALO, POOL_WIDTH), F32)],
        compiler_params=_params(("arbitrary",), block_bytes),
        name="pool_merge",
    )(u, u, u, w_pool, pool_scale.reshape(1, D_MODEL), gates, attn)


def _ffn_kernel(h_ref, wg_ref, wu_ref, wd_ref, o_ref):
    f = pl.program_id(1)
    h = h_ref[...]
    g = jnp.dot(h, wg_ref[...], preferred_element_type=F32)
    u = jnp.dot(h, wu_ref[...], preferred_element_type=F32)
    a = (g * jax.nn.sigmoid(g) * u).astype(BF16)
    part = jnp.dot(a, wd_ref[...], preferred_element_type=F32)

    @pl.when(f == 0)
    def _():
        o_ref[...] = part

    @pl.when(f > 0)
    def _():
        o_ref[...] += part


def _ffn(h, wg, wu, wd, tm=512, tf=256):
    s, d = h.shape
    dff = wg.shape[1]
    assert dff % tf == 0
    block_bytes = 2 * (tm * d * 2 + 3 * d * tf * 2 + tm * d * 4)
    return pl.pallas_call(
        _ffn_kernel,
        grid=(s // tm, dff // tf),
        in_specs=[pl.BlockSpec((tm, d), lambda i, f: (i, 0)),
                  pl.BlockSpec((d, tf), lambda i, f: (0, f)),
                  pl.BlockSpec((d, tf), lambda i, f: (0, f)),
                  pl.BlockSpec((tf, d), lambda i, f: (f, 0))],
        out_specs=pl.BlockSpec((tm, d), lambda i, f: (i, 0)),
        out_shape=jax.ShapeDtypeStruct((s, d), F32),
        compiler_params=_params(("arbitrary", "arbitrary"), block_bytes),
        name="swiglu_ffn",
    )(h, wg, wu, wd)


def _rope_tables(seq):
    rows = seq // GRID_W
    row = jnp.broadcast_to(jnp.arange(rows)[:, None], (rows, GRID_W)).reshape(seq)
    col = jnp.broadcast_to(jnp.arange(GRID_W)[None, :], (rows, GRID_W)).reshape(seq)
    inv_freq = ROPE_THETA ** (-jnp.arange(0, AXIS_ROT_DIM, 2, dtype=F32) / AXIS_ROT_DIM)
    pos = jnp.stack([row, col], axis=-1).astype(F32)
    ang = pos[:, :, None] * inv_freq[None, None, :]
    cos, sin = jnp.cos(ang), jnp.sin(ang)
    cos_t = jnp.concatenate([cos, cos], axis=-1).reshape(seq, HEAD_DIM)
    sin_t = jnp.concatenate([-sin, sin], axis=-1).reshape(seq, HEAD_DIM)
    return cos_t, sin_t


def kernel(x, p, w_in, q_norm, k_norm, w_pool, pool_scale, w_out, norm_mix_pre, norm_mix_post,
           norm_ffn_pre, norm_ffn_post, w_ffn_gate, w_ffn_up, w_ffn_down, w_ple_in, w_ple_gate,
           norm_ple):
    b, s, d = x.shape
    assert b == 1 and d == D_MODEL and s % GRID_W == 0
    cos_t, sin_t = _rope_tables(s)
    xs = x.reshape(s, d)
    h = _norm_cast(xs, norm_mix_pre[0])
    for i in range(DEPTH):
        w_in_b = w_in[i].astype(BF16)
        q = _proj(h, w_in_b, COL_Q, ATTN_WIDTH, BF16, "qk",
                  (q_norm[i] * (HEAD_DIM ** -0.5 * LOG2_E), cos_t, sin_t))
        k = _proj(h, w_in_b, COL_K, KV_WIDTH, BF16, "qk", (k_norm[i], cos_t, sin_t))
        vt = _proj(h, w_in_b, COL_V, KV_WIDTH, BF16, "transpose")
        u = _proj(h, w_in_b, COL_U, POOL_WIDTH, F32)
        gates = _proj(h, w_in_b, COL_GATES, 2 * D_MODEL, F32, "sigmoid")
        attn = _attention(q, k, vt)
        merged = _pool_merge(u, w_pool[i].astype(BF16), pool_scale[i], gates, attn)
        o = _proj(merged, w_out[i].astype(BF16), 0, D_MODEL, F32)
        xs, h = _resnorm(xs, o, norm_mix_post[i], norm_ffn_pre[i], "norm")
        f = _ffn(h, w_ffn_gate[i].astype(BF16), w_ffn_up[i].astype(BF16), w_ffn_down[i].astype(BF16))
        xs, xb = _resnorm(xs, f, norm_ffn_post[i], None, "cast")
        ge = _proj(xb, w_ple_gate[i].astype(BF16), 0, D_MODEL, F32, "ple",
                   (p[i].reshape(s, D_PLE).astype(BF16), w_ple_in[i].astype(BF16)))
        if i + 1 < DEPTH:
            xs, h = _resnorm(xs, ge, norm_ple[i], norm_mix_pre[i + 1], "norm")
        else:
            xs, _ = _resnorm(xs, ge, norm_ple[i], None, "none")
    return xs.reshape(b, s, d)
```

```python
import functools

import jax
import jax.numpy as jnp
from jax import lax
from jax.experimental import pallas as pl
from jax.experimental.pallas import tpu as pltpu

D_MODEL = 4096
SEQ = 8192
DEPTH = 2
N_HEADS = 32
N_KV_HEADS = 8
HEAD_DIM = 128
Q_GROUP = N_HEADS // N_KV_HEADS
ATTN_WIDTH = N_HEADS * HEAD_DIM
KV_WIDTH = N_KV_HEADS * HEAD_DIM
AXIS_ROT_DIM = HEAD_DIM // 2
ROT_HALF = AXIS_ROT_DIM // 2
ROPE_THETA = 10000.0
GRID_W = 64
POOL_WINDOWS = (2, 4, 8, 16)
N_POOL_GROUPS = 4
POOL_WIDTH = 2048
POOL_GROUP_IN = POOL_WIDTH // N_POOL_GROUPS
POOL_GROUP_OUT = D_MODEL // N_POOL_GROUPS
D_FF = 11008
D_PLE = 256
EPS = 1e-6
LOG2_E = 1.4426950408889634

COL_Q = 0
COL_K = COL_Q + ATTN_WIDTH
COL_V = COL_K + KV_WIDTH
COL_U = COL_V + KV_WIDTH
COL_GATES = COL_U + POOL_WIDTH

V7X_VMEM_BYTES = 64 * 1024 * 1024
V7X_SUBLANES = 8
POOL_HALO = 8
SOFTMAX_ROWS = 32
VMEM_TEMP_BYTES = 12 * 1024 * 1024

F32 = jnp.float32
BF16 = jnp.bfloat16


def _params(semantics, block_bytes):
    limit = min(int(block_bytes) + VMEM_TEMP_BYTES, V7X_VMEM_BYTES - 4 * 1024 * 1024)
    return pltpu.CompilerParams(dimension_semantics=semantics, vmem_limit_bytes=limit)


def _rms(x, gain):
    ms = jnp.mean(x * x, axis=-1, keepdims=True)
    return x * lax.rsqrt(ms + EPS) * gain


def _norm_cast_kernel(x_ref, g_ref, o_ref):
    o_ref[...] = _rms(x_ref[...], g_ref[...]).astype(o_ref.dtype)


def _norm_cast(x, gain, tm=256):
    s, d = x.shape
    return pl.pallas_call(
        _norm_cast_kernel,
        grid=(s // tm,),
        in_specs=[pl.BlockSpec((tm, d), lambda i: (i, 0)),
                  pl.BlockSpec((1, d), lambda i: (0, 0))],
        out_specs=pl.BlockSpec((tm, d), lambda i: (i, 0)),
        out_shape=jax.ShapeDtypeStruct((s, d), BF16),
        compiler_params=_params(("arbitrary",), 2 * tm * d * 6),
        name="norm_cast",
    )(x, gain.reshape(1, d))


def _resnorm_kernel(x_ref, f_ref, gpost_ref, *rest, mode):
    xn = x_ref[...] + _rms(f_ref[...], gpost_ref[...])
    if mode == "norm":
        gnext_ref, xo_ref, ho_ref = rest
        ho_ref[...] = _rms(xn, gnext_ref[...]).astype(ho_ref.dtype)
    elif mode == "cast":
        xo_ref, ho_ref = rest
        ho_ref[...] = xn.astype(ho_ref.dtype)
    else:
        (xo_ref,) = rest
    xo_ref[...] = xn


def _resnorm(x, f, gpost, gnext=None, mode="none", tm=256):
    s, d = x.shape
    row = pl.BlockSpec((tm, d), lambda i: (i, 0))
    vec = pl.BlockSpec((1, d), lambda i: (0, 0))
    in_specs = [row, row, vec]
    args = [x, f, gpost.reshape(1, d)]
    if mode == "norm":
        in_specs.append(vec)
        args.append(gnext.reshape(1, d))
    out_shape = [jax.ShapeDtypeStruct((s, d), F32)]
    out_specs = [row]
    if mode != "none":
        out_shape.append(jax.ShapeDtypeStruct((s, d), BF16))
        out_specs.append(row)
    out = pl.pallas_call(
        functools.partial(_resnorm_kernel, mode=mode),
        grid=(s // tm,),
        in_specs=in_specs,
        out_specs=out_specs,
        out_shape=out_shape,
        compiler_params=_params(("arbitrary",), 2 * tm * d * 14),
        name="resnorm_" + mode,
    )(*args)
    return out if mode != "none" else (out[0], None)


def _proj_kernel(a_ref, w_ref, *rest, epilogue):
    acc = jnp.dot(a_ref[...], w_ref[...], preferred_element_type=F32)
    if epilogue == "none":
        (o_ref,) = rest
        o_ref[...] = acc.astype(o_ref.dtype)
    elif epilogue == "transpose":
        (o_ref,) = rest
        o_ref[...] = acc.T.astype(o_ref.dtype)
    elif epilogue == "sigmoid":
        (o_ref,) = rest
        o_ref[...] = jax.nn.sigmoid(acc).astype(o_ref.dtype)
    elif epilogue == "ple":
        p_ref, wp_ref, o_ref = rest
        e = jnp.dot(p_ref[...], wp_ref[...], preferred_element_type=F32)
        o_ref[...] = (jax.nn.sigmoid(acc) * e).astype(o_ref.dtype)
    elif epilogue == "qk":
        g_ref, cos_ref, sin_ref, o_ref, acc_ref = rest
        acc_ref[...] = acc
        lane = lax.broadcasted_iota(jnp.int32, (1, HEAD_DIM), 1)
        first_half = (lane & ROT_HALF) == 0
        gain = g_ref[...]
        cos = cos_ref[...]
        sin = sin_ref[...]
        for h in range(acc_ref.shape[1] // HEAD_DIM):
            sl = slice(h * HEAD_DIM, (h + 1) * HEAD_DIM)
            y = _rms(acc_ref[:, sl], gain)
            partner = jnp.where(first_half,
                                pltpu.roll(y, HEAD_DIM - ROT_HALF, 1),
                                pltpu.roll(y, ROT_HALF, 1))
            o_ref[:, sl] = (y * cos + partner * sin).astype(o_ref.dtype)
    else:
        raise ValueError(epilogue)


def _proj(a, w, col0, ncols, out_dtype, epilogue="none", extra=(), tm=1024, tn=1024):
    m, k = a.shape
    tn = min(tn, ncols)
    assert m % tm == 0 and ncols % tn == 0 and col0 % tn == 0 and w.shape[0] == k
    cb = col0 // tn
    in_specs = [pl.BlockSpec((tm, k), lambda i, j: (i, 0)),
                pl.BlockSpec((k, tn), lambda i, j: (0, cb + j))]
    args = [a, w]
    scratch = []
    extra_bytes = 0
    if epilogue == "qk":
        gain, cos, sin = extra
        in_specs += [pl.BlockSpec((1, HEAD_DIM), lambda i, j: (0, 0)),
                     pl.BlockSpec((tm, HEAD_DIM), lambda i, j: (i, 0)),
                     pl.BlockSpec((tm, HEAD_DIM), lambda i, j: (i, 0))]
        args += [gain.reshape(1, HEAD_DIM), cos, sin]
        scratch = [pltpu.VMEM((tm, tn), F32)]
        extra_bytes = tm * tn * 4 + 4 * tm * HEAD_DIM * 4
    elif epilogue == "ple":
        p, wp = extra
        kp = p.shape[1]
        in_specs += [pl.BlockSpec((tm, kp), lambda i, j: (i, 0)),
                     pl.BlockSpec((kp, tn), lambda i, j: (0, j))]
        args += [p, wp]
        extra_bytes = 2 * (tm * kp + kp * tn) * 2
    out_bytes = jnp.dtype(out_dtype).itemsize
    block_bytes = 2 * (tm * k * 2 + k * tn * 2 + tm * tn * out_bytes) + extra_bytes
    if epilogue == "transpose":
        out_spec = pl.BlockSpec((tn, tm), lambda i, j: (j, i))
        out_shape = jax.ShapeDtypeStruct((ncols, m), out_dtype)
    else:
        out_spec = pl.BlockSpec((tm, tn), lambda i, j: (i, j))
        out_shape = jax.ShapeDtypeStruct((m, ncols), out_dtype)
    return pl.pallas_call(
        functools.partial(_proj_kernel, epilogue=epilogue),
        grid=(m // tm, ncols // tn),
        in_specs=in_specs,
        out_specs=out_spec,
        out_shape=out_shape,
        scratch_shapes=scratch,
        compiler_params=_params(("arbitrary", "arbitrary"), block_bytes),
        name="proj_" + epilogue,
    )(*args)


def _flash_kernel(q_ref, k_ref, vt_ref, o_ref, st_ref, pt_ref, acc_ref, *, tk):
    tq = q_ref.shape[0]
    nchunks = k_ref.shape[0] // tk
    qs = [q_ref[:, h * HEAD_DIM:(h + 1) * HEAD_DIM] for h in range(Q_GROUP)]

    def scores(c):
        kc = k_ref[pl.ds(pl.multiple_of(c * tk, tk), tk), :]
        for h in range(Q_GROUP):
            st_ref[h] = lax.dot_general(kc, qs[h], (((1,), (1,)), ((), ())),
                                        preferred_element_type=F32)

    def softmax_update(ms, ls, slot):
        out = []
        nblk = tk // SOFTMAX_ROWS
        for h in range(Q_GROUP):
            blocks = [st_ref.at[h, r * SOFTMAX_ROWS:(r + 1) * SOFTMAX_ROWS, :] for r in range(nblk)]
            part = blocks[0][...]
            for blk in blocks[1:]:
                part = jnp.maximum(part, blk[...])
            m_new = jnp.maximum(ms[h], jnp.max(part, axis=0, keepdims=True))
            alpha = jnp.exp2(ms[h] - m_new)
            lpart = None
            for r, blk in enumerate(blocks):
                pt = jnp.exp2(blk[...] - m_new)
                pt_ref[slot, h, r * SOFTMAX_ROWS:(r + 1) * SOFTMAX_ROWS, :] = pt.astype(BF16)
                lpart = pt if lpart is None else lpart + pt
            l_new = alpha * ls[h] + jnp.sum(lpart, axis=0, keepdims=True)
            out.append((m_new, l_new, alpha))
        return tuple(zip(*out))

    def value_update(c, alphas, slot):
        vc = vt_ref[:, pl.ds(pl.multiple_of(c * tk, tk), tk)]
        for h in range(Q_GROUP):
            acc_ref[h] = alphas[h] * acc_ref[h] + jnp.dot(vc, pt_ref[slot, h],
                                                         preferred_element_type=F32)

    def stage(c, carry, slot):
        ms, ls, alphas = carry
        scores(c)
        value_update(c - 1, alphas, 1 - slot)
        return softmax_update(ms, ls, slot)

    assert nchunks % 2 == 0
    ms = tuple(jnp.full((1, tq), -jnp.inf, F32) for _ in range(Q_GROUP))
    ls = tuple(jnp.zeros((1, tq), F32) for _ in range(Q_GROUP))
    acc_ref[...] = jnp.zeros(acc_ref.shape, F32)
    scores(0)
    carry = softmax_update(ms, ls, 0)

    def body(t, carry):
        carry = stage(2 * t + 1, carry, 1)
        return stage(2 * t + 2, carry, 0)

    carry = lax.fori_loop(0, (nchunks - 2) // 2, body, carry)
    ms, ls, alphas = stage(nchunks - 1, carry, 1)
    value_update(nchunks - 1, alphas, 1)
    for h in range(Q_GROUP):
        o_ref[:, h * HEAD_DIM:(h + 1) * HEAD_DIM] = (acc_ref[h] / ls[h]).T.astype(o_ref.dtype)


def _attention(q, k, vt, tq=512, tk=512):
    s = q.shape[0]
    gw = Q_GROUP * HEAD_DIM
    block_bytes = (2 * (tq * gw * 2 + 2 * s * HEAD_DIM * 2 + tq * gw * 4)
                   + Q_GROUP * (tk * tq * (4 + 2 * 2) + HEAD_DIM * tq * 4))
    return pl.pallas_call(
        functools.partial(_flash_kernel, tk=tk),
        grid=(N_KV_HEADS, s // tq),
        in_specs=[pl.BlockSpec((tq, gw), lambda g, i: (i, g)),
                  pl.BlockSpec((s, HEAD_DIM), lambda g, i: (0, g)),
                  pl.BlockSpec((HEAD_DIM, s), lambda g, i: (g, 0))],
        out_specs=pl.BlockSpec((tq, gw), lambda g, i: (i, g)),
        out_shape=jax.ShapeDtypeStruct((s, ATTN_WIDTH), F32),
        scratch_shapes=[pltpu.VMEM((Q_GROUP, tk, tq), F32),
                        pltpu.VMEM((2, Q_GROUP, tk, tq), BF16),
                        pltpu.VMEM((Q_GROUP, HEAD_DIM, tq), F32)],
        compiler_params=_params(("arbitrary", "arbitrary"), block_bytes),
        name="flash_attention",
    )(q, k, vt)


def _pool_merge_kernel(up_ref, uc_ref, un_ref, wp_ref, ps_ref, gates_ref, attn_ref, o_ref, ext_ref,
                       *, seq):
    i = pl.program_id(0)
    tm = uc_ref.shape[0]
    last = pl.num_programs(0) - 1
    ext_ref[0:POOL_HALO, :] = jnp.where(i > 0, up_ref[...], 0.0)
    ext_ref[POOL_HALO:POOL_HALO + tm, :] = uc_ref[...]
    ext_ref[POOL_HALO + tm:, :] = jnp.where(i < last, un_ref[...], 0.0)
    t = i * tm + lax.broadcasted_iota(jnp.int32, (tm, 1), 0)
    for g in range(N_POOL_GROUPS):
        half = POOL_WINDOWS[g] // 2
        cols = slice(g * POOL_GROUP_IN, (g + 1) * POOL_GROUP_IN)
        wsum = ext_ref[POOL_HALO - half:POOL_HALO - half + tm, cols]
        for d in range(-half + 1, half):
            wsum = wsum + ext_ref[POOL_HALO + d:POOL_HALO + d + tm, cols]
        count = (jnp.minimum(t + half, seq) - jnp.maximum(t - half, 0)).astype(F32)
        delta = wsum / count - uc_ref[:, cols]
        y = jnp.dot(delta.astype(BF16), wp_ref[g], preferred_element_type=F32)
        oc = slice(g * POOL_GROUP_OUT, (g + 1) * POOL_GROUP_OUT)
        pool = y * ps_ref[:, oc]
        ga = gates_ref[:, oc]
        gp = gates_ref[:, D_MODEL + g * POOL_GROUP_OUT:D_MODEL + (g + 1) * POOL_GROUP_OUT]
        o_ref[:, oc] = (ga * attn_ref[:, oc] + gp * pool).astype(o_ref.dtype)


def _pool_merge(u, w_pool, pool_scale, gates, attn, tm=256):
    s = u.shape[0]
    hb = tm // POOL_HALO
    nhb = s // POOL_HALO
    block_bytes = (2 * (tm * POOL_WIDTH * 4 + 2 * POOL_HALO * POOL_WIDTH * 4 + tm * 2 * D_MODEL * 4
                        + tm * D_MODEL * 4 + tm * D_MODEL * 2 + w_pool.size * 2)
                   + (tm + 2 * POOL_HALO) * POOL_WIDTH * 4)
    return pl.pallas_call(
        functools.partial(_pool_merge_kernel, seq=s),
        grid=(s // tm,),
        in_specs=[pl.BlockSpec((POOL_HALO, POOL_WIDTH), lambda i: (jnp.maximum(i * hb - 1, 0), 0)),
                  pl.BlockSpec((tm, POOL_WIDTH), lambda i: (i, 0)),
                  pl.BlockSpec((POOL_HALO, POOL_WIDTH), lambda i: (jnp.minimum((i + 1) * hb, nhb - 1), 0)),
                  pl.BlockSpec(w_pool.shape, lambda i: (0, 0, 0)),
                  pl.BlockSpec((1, D_MODEL), lambda i: (0, 0)),
                  pl.BlockSpec((tm, 2 * D_MODEL), lambda i: (i, 0)),
                  pl.BlockSpec((tm, D_MODEL), lambda i: (i, 0))],
        out_specs=pl.BlockSpec((tm, D_MODEL), lambda i: (i, 0)),
        out_shape=jax.ShapeDtypeStruct((s, D_MODEL), BF16),
        scratch_shapes=[pltpu.VMEM((tm + 2 * POOL_HALO, POOL_WIDTH), F32)],
        compiler_params=_params(("arbitrary",), block_bytes),
        name="pool_merge",
    )(u, u, u, w_pool, pool_scale.reshape(1, D_MODEL), gates, attn)


def _ffn_kernel(h_ref, wg_ref, wu_ref, wd_ref, o_ref):
    f = pl.program_id(1)
    h = h_ref[...]
    g = jnp.dot(h, wg_ref[...], preferred_element_type=F32)
    u = jnp.dot(h, wu_ref[...], preferred_element_type=F32)
    a = (g * jax.nn.sigmoid(g) * u).astype(BF16)
    part = jnp.dot(a, wd_ref[...], preferred_element_type=F32)

    @pl.when(f == 0)
    def _():
        o_ref[...] = part

    @pl.when(f > 0)
    def _():
        o_ref[...] += part


def _ffn(h, wg, wu, wd, tm=512, tf=256):
    s, d = h.shape
    dff = wg.shape[1]
    assert dff % tf == 0
    block_bytes = 2 * (tm * d * 2 + 3 * d * tf * 2 + tm * d * 4)
    return pl.pallas_call(
        _ffn_kernel,
        grid=(s // tm, dff // tf),
        in_specs=[pl.BlockSpec((tm, d), lambda i, f: (i, 0)),
                  pl.BlockSpec((d, tf), lambda i, f: (0, f)),
                  pl.BlockSpec((d, tf), lambda i, f: (0, f)),
                  pl.BlockSpec((tf, d), lambda i, f: (f, 0))],
        out_specs=pl.BlockSpec((tm, d), lambda i, f: (i, 0)),
        out_shape=jax.ShapeDtypeStruct((s, d), F32),
        compiler_params=_params(("arbitrary", "arbitrary"), block_bytes),
        name="swiglu_ffn",
    )(h, wg, wu, wd)


def _rope_tables(seq):
    rows = seq // GRID_W
    row = jnp.broadcast_to(jnp.arange(rows)[:, None], (rows, GRID_W)).reshape(seq)
    col = jnp.broadcast_to(jnp.arange(GRID_W)[None, :], (rows, GRID_W)).reshape(seq)
    inv_freq = ROPE_THETA ** (-jnp.arange(0, AXIS_ROT_DIM, 2, dtype=F32) / AXIS_ROT_DIM)
    pos = jnp.stack([row, col], axis=-1).astype(F32)
    ang = pos[:, :, None] * inv_freq[None, None, :]
    cos, sin = jnp.cos(ang), jnp.sin(ang)
    cos_t = jnp.concatenate([cos, cos], axis=-1).reshape(seq, HEAD_DIM)
    sin_t = jnp.concatenate([-sin, sin], axis=-1).reshape(seq, HEAD_DIM)
    return cos_t, sin_t


def kernel(x, p, w_in, q_norm, k_norm, w_pool, pool_scale, w_out, norm_mix_pre, norm_mix_post,
           norm_ffn_pre, norm_ffn_post, w_ffn_gate, w_ffn_up, w_ffn_down, w_ple_in, w_ple_gate,
           norm_ple):
    b, s, d = x.shape
    assert b == 1 and d == D_MODEL and s % GRID_W == 0
    cos_t, sin_t = _rope_tables(s)
    xs = x.reshape(s, d)
    h = _norm_cast(xs, norm_mix_pre[0])
    for i in range(DEPTH):
        w_in_b = w_in[i].astype(BF16)
        q = _proj(h, w_in_b, COL_Q, ATTN_WIDTH, BF16, "qk",
                  (q_norm[i] * (HEAD_DIM ** -0.5 * LOG2_E), cos_t, sin_t))
        k = _proj(h, w_in_b, COL_K, KV_WIDTH, BF16, "qk", (k_norm[i], cos_t, sin_t))
        vt = _proj(h, w_in_b, COL_V, KV_WIDTH, BF16, "transpose")
        u = _proj(h, w_in_b, COL_U, POOL_WIDTH, F32)
        gates = _proj(h, w_in_b, COL_GATES, 2 * D_MODEL, F32, "sigmoid")
        attn = _attention(q, k, vt)
        merged = _pool_merge(u, w_pool[i].astype(BF16), pool_scale[i], gates, attn)
        o = _proj(merged, w_out[i].astype(BF16), 0, D_MODEL, F32)
        xs, h = _resnorm(xs, o, norm_mix_post[i], norm_ffn_pre[i], "norm")
        f = _ffn(h, w_ffn_gate[i].astype(BF16), w_ffn_up[i].astype(BF16), w_ffn_down[i].astype(BF16))
        xs, xb = _resnorm(xs, f, norm_ffn_post[i], None, "cast")
        ge = _proj(xb, w_ple_gate[i].astype(BF16), 0, D_MODEL, F32, "ple",
                   (p[i].reshape(s, D_PLE).astype(BF16), w_ple_in[i].astype(BF16)))
        if i + 1 < DEPTH:
            xs, h = _resnorm(xs, ge, norm_ple[i], norm_mix_pre[i + 1], "norm")
        else:
            xs, _ = _resnorm(xs, ge, norm_ple[i], None, "none")
    return xs.reshape(b, s, d)
```

```python
import functools

import jax
import jax.numpy as jnp
from jax import lax
from jax.experimental import pallas as pl
from jax.experimental.pallas import tpu as pltpu

D_MODEL = 4096
SEQ = 8192
DEPTH = 2
N_HEADS = 32
N_KV_HEADS = 8
HEAD_DIM = 128
Q_GROUP = N_HEADS // N_KV_HEADS
ATTN_WIDTH = N_HEADS * HEAD_DIM
KV_WIDTH = N_KV_HEADS * HEAD_DIM
AXIS_ROT_DIM = HEAD_DIM // 2
ROT_HALF = AXIS_ROT_DIM // 2
ROPE_THETA = 10000.0
GRID_W = 64
POOL_WINDOWS = (2, 4, 8, 16)
N_POOL_GROUPS = 4
POOL_WIDTH = 2048
POOL_GROUP_IN = POOL_WIDTH // N_POOL_GROUPS
POOL_GROUP_OUT = D_MODEL // N_POOL_GROUPS
D_FF = 11008
D_PLE = 256
EPS = 1e-6
LOG2_E = 1.4426950408889634

COL_Q = 0
COL_K = COL_Q + ATTN_WIDTH
COL_V = COL_K + KV_WIDTH
COL_U = COL_V + KV_WIDTH
COL_GATES = COL_U + POOL_WIDTH

V7X_VMEM_BYTES = 64 * 1024 * 1024
V7X_SUBLANES = 8
V7X_LANES = 128
BF16_SUBLANES = 2 * V7X_SUBLANES
POOL_HALO = 8
SOFTMAX_ROWS = 32
VMEM_TEMP_BYTES = 12 * 1024 * 1024

F32 = jnp.float32
BF16 = jnp.bfloat16


def _params(semantics, block_bytes):
    limit = min(int(block_bytes) + VMEM_TEMP_BYTES, V7X_VMEM_BYTES - 4 * 1024 * 1024)
    return pltpu.CompilerParams(dimension_semantics=semantics, vmem_limit_bytes=limit)


def _rms(x, gain):
    ms = jnp.mean(x * x, axis=-1, keepdims=True)
    return x * lax.rsqrt(ms + EPS) * gain


def _norm_cast_kernel(x_ref, g_ref, o_ref):
    o_ref[...] = _rms(x_ref[...], g_ref[...]).astype(o_ref.dtype)


def _norm_cast(x, gain, tm=256):
    s, d = x.shape
    return pl.pallas_call(
        _norm_cast_kernel,
        grid=(s // tm,),
        in_specs=[pl.BlockSpec((tm, d), lambda i: (i, 0)),
                  pl.BlockSpec((1, d), lambda i: (0, 0))],
        out_specs=pl.BlockSpec((tm, d), lambda i: (i, 0)),
        out_shape=jax.ShapeDtypeStruct((s, d), BF16),
        compiler_params=_params(("arbitrary",), 2 * tm * d * 6),
        name="norm_cast",
    )(x, gain.reshape(1, d))


def _resnorm_kernel(x_ref, f_ref, gpost_ref, *rest, mode):
    xn = x_ref[...] + _rms(f_ref[...], gpost_ref[...])
    if mode == "norm":
        gnext_ref, xo_ref, ho_ref = rest
        ho_ref[...] = _rms(xn, gnext_ref[...]).astype(ho_ref.dtype)
    elif mode == "cast":
        xo_ref, ho_ref = rest
        ho_ref[...] = xn.astype(ho_ref.dtype)
    else:
        (xo_ref,) = rest
    xo_ref[...] = xn


def _resnorm(x, f, gpost, gnext=None, mode="none", tm=256):
    s, d = x.shape
    row = pl.BlockSpec((tm, d), lambda i: (i, 0))
    vec = pl.BlockSpec((1, d), lambda i: (0, 0))
    in_specs = [row, row, vec]
    args = [x, f, gpost.reshape(1, d)]
    if mode == "norm":
        in_specs.append(vec)
        args.append(gnext.reshape(1, d))
    out_shape = [jax.ShapeDtypeStruct((s, d), F32)]
    out_specs = [row]
    if mode != "none":
        out_shape.append(jax.ShapeDtypeStruct((s, d), BF16))
        out_specs.append(row)
    out = pl.pallas_call(
        functools.partial(_resnorm_kernel, mode=mode),
        grid=(s // tm,),
        in_specs=in_specs,
        out_specs=out_specs,
        out_shape=out_shape,
        compiler_params=_params(("arbitrary",), 2 * tm * d * 14),
        name="resnorm_" + mode,
    )(*args)
    return out if mode != "none" else (out[0], None)


def _proj_kernel(a_ref, w_ref, *rest, epilogue):
    acc = jnp.dot(a_ref[...], w_ref[...], preferred_element_type=F32)
    if epilogue == "none":
        (o_ref,) = rest
        o_ref[...] = acc.astype(o_ref.dtype)
    elif epilogue == "transpose":
        (o_ref,) = rest
        o_ref[...] = acc.T.astype(o_ref.dtype)
    elif epilogue == "sigmoid":
        (o_ref,) = rest
        o_ref[...] = jax.nn.sigmoid(acc).astype(o_ref.dtype)
    elif epilogue == "ple":
        p_ref, wp_ref, o_ref = rest
        e = jnp.dot(p_ref[...], wp_ref[...], preferred_element_type=F32)
        o_ref[...] = (jax.nn.sigmoid(acc) * e).astype(o_ref.dtype)
    elif epilogue == "qk":
        g_ref, cos_ref, sin_ref, o_ref, acc_ref = rest
        acc_ref[...] = acc
        lane = lax.broadcasted_iota(jnp.int32, (1, HEAD_DIM), 1)
        first_half = (lane & ROT_HALF) == 0
        gain = g_ref[...]
        cos = cos_ref[...]
        sin = sin_ref[...]
        for h in range(acc_ref.shape[1] // HEAD_DIM):
            sl = slice(h * HEAD_DIM, (h + 1) * HEAD_DIM)
            y = _rms(acc_ref[:, sl], gain)
            partner = jnp.where(first_half,
                                pltpu.roll(y, HEAD_DIM - ROT_HALF, 1),
                                pltpu.roll(y, ROT_HALF, 1))
            o_ref[:, sl] = (y * cos + partner * sin).astype(o_ref.dtype)
    else:
        raise ValueError(epilogue)


def _proj(a, w, col0, ncols, out_dtype, epilogue="none", extra=(), tm=1024, tn=1024):
    m, k = a.shape
    tn = min(tn, ncols)
    assert m % tm == 0 and ncols % tn == 0 and col0 % tn == 0 and w.shape[0] == k
    cb = col0 // tn
    in_specs = [pl.BlockSpec((tm, k), lambda i, j: (i, 0)),
                pl.BlockSpec((k, tn), lambda i, j: (0, cb + j))]
    args = [a, w]
    scratch = []
    extra_bytes = 0
    if epilogue == "qk":
        gain, cos, sin = extra
        in_specs += [pl.BlockSpec((1, HEAD_DIM), lambda i, j: (0, 0)),
                     pl.BlockSpec((tm, HEAD_DIM), lambda i, j: (i, 0)),
                     pl.BlockSpec((tm, HEAD_DIM), lambda i, j: (i, 0))]
        args += [gain.reshape(1, HEAD_DIM), cos, sin]
        scratch = [pltpu.VMEM((tm, tn), F32)]
        extra_bytes = tm * tn * 4 + 4 * tm * HEAD_DIM * 4
    elif epilogue == "ple":
        p, wp = extra
        kp = p.shape[1]
        in_specs += [pl.BlockSpec((tm, kp), lambda i, j: (i, 0)),
                     pl.BlockSpec((kp, tn), lambda i, j: (0, j))]
        args += [p, wp]
        extra_bytes = 2 * (tm * kp + kp * tn) * 2
    out_bytes = jnp.dtype(out_dtype).itemsize
    block_bytes = 2 * (tm * k * 2 + k * tn * 2 + tm * tn * out_bytes) + extra_bytes
    if epilogue == "transpose":
        out_spec = pl.BlockSpec((tn, tm), lambda i, j: (j, i))
        out_shape = jax.ShapeDtypeStruct((ncols, m), out_dtype)
    else:
        out_spec = pl.BlockSpec((tm, tn), lambda i, j: (i, j))
        out_shape = jax.ShapeDtypeStruct((m, ncols), out_dtype)
    return pl.pallas_call(
        functools.partial(_proj_kernel, epilogue=epilogue),
        grid=(m // tm, ncols // tn),
        in_specs=in_specs,
        out_specs=out_spec,
        out_shape=out_shape,
        scratch_shapes=scratch,
        compiler_params=_params(("arbitrary", "arbitrary"), block_bytes),
        name="proj_" + epilogue,
    )(*args)


def _flash_kernel(q_ref, k_ref, vt_ref, *rest, tk, n_cast):
    cast_in = rest[:n_cast]
    o_ref = rest[n_cast]
    cast_out = rest[n_cast + 1:2 * n_cast + 1]
    st_ref, pt_ref, acc_ref = rest[2 * n_cast + 1:]
    for src, dst in zip(cast_in, cast_out):
        dst[...] = src[...].astype(dst.dtype)
    tq = q_ref.shape[0]
    nchunks = k_ref.shape[0] // tk
    qs = [q_ref[:, h * HEAD_DIM:(h + 1) * HEAD_DIM] for h in range(Q_GROUP)]

    def scores(c):
        kc = k_ref[pl.ds(pl.multiple_of(c * tk, tk), tk), :]
        for h in range(Q_GROUP):
            st_ref[h] = lax.dot_general(kc, qs[h], (((1,), (1,)), ((), ())),
                                        preferred_element_type=F32)

    def softmax_update(ms, ls, slot):
        out = []
        nblk = tk // SOFTMAX_ROWS
        for h in range(Q_GROUP):
            blocks = [st_ref.at[h, r * SOFTMAX_ROWS:(r + 1) * SOFTMAX_ROWS, :] for r in range(nblk)]
            part = blocks[0][...]
            for blk in blocks[1:]:
                part = jnp.maximum(part, blk[...])
            m_new = jnp.maximum(ms[h], jnp.max(part, axis=0, keepdims=True))
            alpha = jnp.exp2(ms[h] - m_new)
            lpart = None
            for r, blk in enumerate(blocks):
                pt = jnp.exp2(blk[...] - m_new)
                pt_ref[slot, h, r * SOFTMAX_ROWS:(r + 1) * SOFTMAX_ROWS, :] = pt.astype(BF16)
                lpart = pt if lpart is None else lpart + pt
            l_new = alpha * ls[h] + jnp.sum(lpart, axis=0, keepdims=True)
            out.append((m_new, l_new, alpha))
        return tuple(zip(*out))

    def value_update(c, alphas, slot):
        vc = vt_ref[:, pl.ds(pl.multiple_of(c * tk, tk), tk)]
        for h in range(Q_GROUP):
            acc_ref[h] = alphas[h] * acc_ref[h] + jnp.dot(vc, pt_ref[slot, h],
                                                         preferred_element_type=F32)

    def stage(c, carry, slot):
        ms, ls, alphas = carry
        scores(c)
        value_update(c - 1, alphas, 1 - slot)
        return softmax_update(ms, ls, slot)

    assert nchunks % 2 == 0
    ms = tuple(jnp.full((1, tq), -jnp.inf, F32) for _ in range(Q_GROUP))
    ls = tuple(jnp.zeros((1, tq), F32) for _ in range(Q_GROUP))
    acc_ref[...] = jnp.zeros(acc_ref.shape, F32)
    scores(0)
    carry = softmax_update(ms, ls, 0)

    def body(t, carry):
        carry = stage(2 * t + 1, carry, 1)
        return stage(2 * t + 2, carry, 0)

    carry = lax.fori_loop(0, (nchunks - 2) // 2, body, carry)
    ms, ls, alphas = stage(nchunks - 1, carry, 1)
    value_update(nchunks - 1, alphas, 1)
    for h in range(Q_GROUP):
        o_ref[:, h * HEAD_DIM:(h + 1) * HEAD_DIM] = (acc_ref[h] / ls[h]).T.astype(o_ref.dtype)


def _cast_block(shape, nsteps):
    rows, cols = shape
    ncol = 1
    while ncol <= nsteps:
        nrow = nsteps // ncol
        if (rows % (nrow * BF16_SUBLANES) == 0 and cols % (ncol * V7X_LANES) == 0
                and nrow * ncol == nsteps):
            return (rows // nrow, cols // ncol), ncol
        ncol *= 2
    raise ValueError(f"cannot tile {shape} over {nsteps} steps")


def _attention(q, k, vt, cast=(), tq=512, tk=512):
    s = q.shape[0]
    gw = Q_GROUP * HEAD_DIM
    nq = s // tq
    nsteps = N_KV_HEADS * nq
    block_bytes = (2 * (tq * gw * 2 + 2 * s * HEAD_DIM * 2 + tq * gw * 4)
                   + Q_GROUP * (tk * tq * (4 + 2 * 2) + HEAD_DIM * tq * 4))
    cast_in_specs, cast_out_specs = [], []
    for w, layer in cast:
        blk, ncol = _cast_block(w.shape[1:], nsteps)
        cast_in_specs.append(pl.BlockSpec(
            (None,) + blk,
            lambda g, i, ncol=ncol, layer=layer: (layer, (g * nq + i) // ncol, (g * nq + i) % ncol)))
        cast_out_specs.append(pl.BlockSpec(
            blk, lambda g, i, ncol=ncol: ((g * nq + i) // ncol, (g * nq + i) % ncol)))
        block_bytes += 2 * blk[0] * blk[1] * (4 + 2)
    out = pl.pallas_call(
        functools.partial(_flash_kernel, tk=tk, n_cast=len(cast)),
        grid=(N_KV_HEADS, nq),
        in_specs=[pl.BlockSpec((tq, gw), lambda g, i: (i, g)),
                  pl.BlockSpec((s, HEAD_DIM), lambda g, i: (0, g)),
                  pl.BlockSpec((HEAD_DIM, s), lambda g, i: (g, 0))] + cast_in_specs,
        out_specs=[pl.BlockSpec((tq, gw), lambda g, i: (i, g))] + cast_out_specs,
        out_shape=[jax.ShapeDtypeStruct((s, ATTN_WIDTH), F32)]
                  + [jax.ShapeDtypeStruct(w.shape[1:], BF16) for w, _ in cast],
        scratch_shapes=[pltpu.VMEM((Q_GROUP, tk, tq), F32),
                        pltpu.VMEM((2, Q_GROUP, tk, tq), BF16),
                        pltpu.VMEM((Q_GROUP, HEAD_DIM, tq), F32)],
        compiler_params=_params(("arbitrary", "arbitrary"), block_bytes),
        name="flash_attention",
    )(q, k, vt, *[w for w, _ in cast])
    return out[0], out[1:]


def _pool_merge_kernel(up_ref, uc_ref, un_ref, wp_ref, ps_ref, gates_ref, attn_ref, o_ref, ext_ref,
                       *, seq):
    i = pl.program_id(0)
    tm = uc_ref.shape[0]
    last = pl.num_programs(0) - 1
    ext_ref[0:POOL_HALO, :] = jnp.where(i > 0, up_ref[...], 0.0)
    ext_ref[POOL_HALO:POOL_HALO + tm, :] = uc_ref[...]
    ext_ref[POOL_HALO + tm:, :] = jnp.where(i < last, un_ref[...], 0.0)
    t = i * tm + lax.broadcasted_iota(jnp.int32, (tm, 1), 0)
    for g in range(N_POOL_GROUPS):
        half = POOL_WINDOWS[g] // 2
        cols = slice(g * POOL_GROUP_IN, (g + 1) * POOL_GROUP_IN)
        wsum = ext_ref[POOL_HALO - half:POOL_HALO - half + tm, cols]
        for d in range(-half + 1, half):
            wsum = wsum + ext_ref[POOL_HALO + d:POOL_HALO + d + tm, cols]
        count = (jnp.minimum(t + half, seq) - jnp.maximum(t - half, 0)).astype(F32)
        delta = wsum / count - uc_ref[:, cols]
        y = jnp.dot(delta.astype(BF16), wp_ref[g], preferred_element_type=F32)
        oc = slice(g * POOL_GROUP_OUT, (g + 1) * POOL_GROUP_OUT)
        pool = y * ps_ref[:, oc]
        ga = gates_ref[:, oc]
        gp = gates_ref[:, D_MODEL + g * POOL_GROUP_OUT:D_MODEL + (g + 1) * POOL_GROUP_OUT]
        o_ref[:, oc] = (ga * attn_ref[:, oc] + gp * pool).astype(o_ref.dtype)


def _pool_merge(u, w_pool, pool_scale, gates, attn, tm=256):
    s = u.shape[0]
    hb = tm // POOL_HALO
    nhb = s // POOL_HALO
    block_bytes = (2 * (tm * POOL_WIDTH * 4 + 2 * POOL_HALO * POOL_WIDTH * 4 + tm * 2 * D_MODEL * 4
                        + tm * D_MODEL * 4 + tm * D_MODEL * 2 + w_pool.size * 2)
                   + (tm + 2 * POOL_HALO) * POOL_WIDTH * 4)
    return pl.pallas_call(
        functools.partial(_pool_merge_kernel, seq=s),
        grid=(s // tm,),
        in_specs=[pl.BlockSpec((POOL_HALO, POOL_WIDTH), lambda i: (jnp.maximum(i * hb - 1, 0), 0)),
                  pl.BlockSpec((tm, POOL_WIDTH), lambda i: (i, 0)),
                  pl.BlockSpec((POOL_HALO, POOL_WIDTH), lambda i: (jnp.minimum((i + 1) * hb, nhb - 1), 0)),
                  pl.BlockSpec(w_pool.shape, lambda i: (0, 0, 0)),
                  pl.BlockSpec((1, D_MODEL), lambda i: (0, 0)),
                  pl.BlockSpec((tm, 2 * D_MODEL), lambda i: (i, 0)),
                  pl.BlockSpec((tm, D_MODEL), lambda i: (i, 0))],
        out_specs=pl.BlockSpec((tm, D_MODEL), lambda i: (i, 0)),
        out_shape=jax.ShapeDtypeStruct((s, D_MODEL), BF16),
        scratch_shapes=[pltpu.VMEM((tm + 2 * POOL_HALO, POOL_WIDTH), F32)],
        compiler_params=_params(("arbitrary",), block_bytes),
        name="pool_merge",
    )(u, u, u, w_pool, pool_scale.reshape(1, D_MODEL), gates, attn)


def _ffn_kernel(h_ref, wg_ref, wu_ref, wd_ref, o_ref):
    @pl.when(pl.program_id(1) == 0)
    def _():
        o_ref[...] = jnp.zeros(o_ref.shape, F32)

    h = h_ref[...]
    g = jnp.dot(h, wg_ref[...], preferred_element_type=F32)
    u = jnp.dot(h, wu_ref[...], preferred_element_type=F32)
    a = (g * jax.nn.sigmoid(g) * u).astype(BF16)
    o_ref[...] += jnp.dot(a, wd_ref[...], preferred_element_type=F32)


def _ffn(h, wg, wu, wd, tm=512, tf=256):
    s, d = h.shape
    dff = wg.shape[1]
    assert dff % tf == 0
    block_bytes = 2 * (tm * d * 2 + 3 * d * tf * 2 + tm * d * 4)
    return pl.pallas_call(
        _ffn_kernel,
        grid=(s // tm, dff // tf),
        in_specs=[pl.BlockSpec((tm, d), lambda i, f: (i, 0)),
                  pl.BlockSpec((d, tf), lambda i, f: (0, f)),
                  pl.BlockSpec((d, tf), lambda i, f: (0, f)),
                  pl.BlockSpec((tf, d), lambda i, f: (f, 0))],
        out_specs=pl.BlockSpec((tm, d), lambda i, f: (i, 0)),
        out_shape=jax.ShapeDtypeStruct((s, d), F32),
        compiler_params=_params(("arbitrary", "arbitrary"), block_bytes),
        name="swiglu_ffn",
    )(h, wg, wu, wd)


def _rope_tables(seq):
    rows = seq // GRID_W
    row = jnp.broadcast_to(jnp.arange(rows)[:, None], (rows, GRID_W)).reshape(seq)
    col = jnp.broadcast_to(jnp.arange(GRID_W)[None, :], (rows, GRID_W)).reshape(seq)
    inv_freq = ROPE_THETA ** (-jnp.arange(0, AXIS_ROT_DIM, 2, dtype=F32) / AXIS_ROT_DIM)
    pos = jnp.stack([row, col], axis=-1).astype(F32)
    ang = pos[:, :, None] * inv_freq[None, None, :]
    cos, sin = jnp.cos(ang), jnp.sin(ang)
    cos_t = jnp.concatenate([cos, cos], axis=-1).reshape(seq, HEAD_DIM)
    sin_t = jnp.concatenate([-sin, sin], axis=-1).reshape(seq, HEAD_DIM)
    return cos_t, sin_t


def kernel(x, p, w_in, q_norm, k_norm, w_pool, pool_scale, w_out, norm_mix_pre, norm_mix_post,
           norm_ffn_pre, norm_ffn_post, w_ffn_gate, w_ffn_up, w_ffn_down, w_ple_in, w_ple_gate,
           norm_ple):
    b, s, d = x.shape
    assert b == 1 and d == D_MODEL and s % GRID_W == 0
    cos_t, sin_t = _rope_tables(s)
    xs = x.reshape(s, d)
    h = _norm_cast(xs, norm_mix_pre[0])
    w_in_b = w_in[0].astype(BF16)
    for i in range(DEPTH):
        q = _proj(h, w_in_b, COL_Q, ATTN_WIDTH, BF16, "qk",
                  (q_norm[i] * (HEAD_DIM ** -0.5 * LOG2_E), cos_t, sin_t))
        k = _proj(h, w_in_b, COL_K, KV_WIDTH, BF16, "qk", (k_norm[i], cos_t, sin_t))
        vt = _proj(h, w_in_b, COL_V, KV_WIDTH, BF16, "transpose")
        u = _proj(h, w_in_b, COL_U, POOL_WIDTH, F32)
        gates = _proj(h, w_in_b, COL_GATES, 2 * D_MODEL, F32, "sigmoid")
        later = [(w_out, i), (w_ffn_gate, i), (w_ffn_up, i), (w_ffn_down, i), (w_ple_gate, i)]
        if i + 1 < DEPTH:
            later.append((w_in, i + 1))
        attn, later_b = _attention(q, k, vt, later)
        w_out_b, w_gate_b, w_up_b, w_down_b, w_ple_gate_b = later_b[:5]
        if i + 1 < DEPTH:
            w_in_b = later_b[5]
        merged = _pool_merge(u, w_pool[i].astype(BF16), pool_scale[i], gates, attn)
        o = _proj(merged, w_out_b, 0, D_MODEL, F32)
        xs, h = _resnorm(xs, o, norm_mix_post[i], norm_ffn_pre[i], "norm")
        f = _ffn(h, w_gate_b, w_up_b, w_down_b)
        xs, xb = _resnorm(xs, f, norm_ffn_post[i], None, "cast")
        ge = _proj(xb, w_ple_gate_b, 0, D_MODEL, F32, "ple",
                   (p[i].reshape(s, D_PLE).astype(BF16), w_ple_in[i].astype(BF16)))
        if i + 1 < DEPTH:
            xs, h = _resnorm(xs, ge, norm_ple[i], norm_mix_pre[i + 1], "norm")
        else:
            xs, _ = _resnorm(xs, ge, norm_ple[i], None, "none")
    return xs.reshape(b, s, d)
```

```python
import functools

import jax
import jax.numpy as jnp
from jax import lax
from jax.experimental import pallas as pl
from jax.experimental.pallas import tpu as pltpu

D_MODEL = 4096
SEQ = 8192
DEPTH = 2
N_HEADS = 32
N_KV_HEADS = 8
HEAD_DIM = 128
Q_GROUP = N_HEADS // N_KV_HEADS
ATTN_WIDTH = N_HEADS * HEAD_DIM
KV_WIDTH = N_KV_HEADS * HEAD_DIM
AXIS_ROT_DIM = HEAD_DIM // 2
ROT_HALF = AXIS_ROT_DIM // 2
ROPE_THETA = 10000.0
GRID_W = 64
POOL_WINDOWS = (2, 4, 8, 16)
N_POOL_GROUPS = 4
POOL_WIDTH = 2048
POOL_GROUP_IN = POOL_WIDTH // N_POOL_GROUPS
POOL_GROUP_OUT = D_MODEL // N_POOL_GROUPS
D_FF = 11008
D_PLE = 256
EPS = 1e-6
LOG2_E = 1.4426950408889634

COL_Q = 0
COL_K = COL_Q + ATTN_WIDTH
COL_V = COL_K + KV_WIDTH
COL_U = COL_V + KV_WIDTH
COL_GATES = COL_U + POOL_WIDTH

V7X_VMEM_BYTES = 64 * 1024 * 1024
V7X_SUBLANES = 8
V7X_LANES = 128
BF16_SUBLANES = 2 * V7X_SUBLANES
POOL_HALO = 8
SOFTMAX_ROWS = 32
S_SLOTS = 2
P_SLOTS = 3
VMEM_TEMP_BYTES = 12 * 1024 * 1024

F32 = jnp.float32
BF16 = jnp.bfloat16


def _params(semantics, block_bytes):
    limit = min(int(block_bytes) + VMEM_TEMP_BYTES, V7X_VMEM_BYTES - 4 * 1024 * 1024)
    return pltpu.CompilerParams(dimension_semantics=semantics, vmem_limit_bytes=limit)


def _rms(x, gain):
    ms = jnp.mean(x * x, axis=-1, keepdims=True)
    return x * lax.rsqrt(ms + EPS) * gain


def _norm_cast_kernel(x_ref, g_ref, o_ref):
    o_ref[...] = _rms(x_ref[...], g_ref[...]).astype(o_ref.dtype)


def _norm_cast(x, gain, tm=256):
    s, d = x.shape
    return pl.pallas_call(
        _norm_cast_kernel,
        grid=(s // tm,),
        in_specs=[pl.BlockSpec((tm, d), lambda i: (i, 0)),
                  pl.BlockSpec((1, d), lambda i: (0, 0))],
        out_specs=pl.BlockSpec((tm, d), lambda i: (i, 0)),
        out_shape=jax.ShapeDtypeStruct((s, d), BF16),
        compiler_params=_params(("arbitrary",), 2 * tm * d * 6),
        name="norm_cast",
    )(x, gain.reshape(1, d))


def _resnorm_kernel(x_ref, f_ref, gpost_ref, *rest, mode):
    xn = x_ref[...] + _rms(f_ref[...], gpost_ref[...])
    if mode == "norm":
        gnext_ref, xo_ref, ho_ref = rest
        ho_ref[...] = _rms(xn, gnext_ref[...]).astype(ho_ref.dtype)
    elif mode == "cast":
        xo_ref, ho_ref = rest
        ho_ref[...] = xn.astype(ho_ref.dtype)
    else:
        (xo_ref,) = rest
    xo_ref[...] = xn


def _resnorm(x, f, gpost, gnext=None, mode="none", tm=256):
    s, d = x.shape
    row = pl.BlockSpec((tm, d), lambda i: (i, 0))
    vec = pl.BlockSpec((1, d), lambda i: (0, 0))
    in_specs = [row, row, vec]
    args = [x, f, gpost.reshape(1, d)]
    if mode == "norm":
        in_specs.append(vec)
        args.append(gnext.reshape(1, d))
    out_shape = [jax.ShapeDtypeStruct((s, d), F32)]
    out_specs = [row]
    if mode != "none":
        out_shape.append(jax.ShapeDtypeStruct((s, d), BF16))
        out_specs.append(row)
    out = pl.pallas_call(
        functools.partial(_resnorm_kernel, mode=mode),
        grid=(s // tm,),
        in_specs=in_specs,
        out_specs=out_specs,
        out_shape=out_shape,
        compiler_params=_params(("arbitrary",), 2 * tm * d * 14),
        name="resnorm_" + mode,
    )(*args)
    return out if mode != "none" else (out[0], None)


def _proj_kernel(a_ref, w_ref, *rest, epilogue):
    acc = jnp.dot(a_ref[...], w_ref[...], preferred_element_type=F32)
    if epilogue == "none":
        (o_ref,) = rest
        o_ref[...] = acc.astype(o_ref.dtype)
    elif epilogue == "transpose":
        (o_ref,) = rest
        o_ref[...] = acc.T.astype(o_ref.dtype)
    elif epilogue == "sigmoid":
        (o_ref,) = rest
        o_ref[...] = jax.nn.sigmoid(acc).astype(o_ref.dtype)
    elif epilogue == "ple":
        p_ref, wp_ref, o_ref = rest
        e = jnp.dot(p_ref[...], wp_ref[...], preferred_element_type=F32)
        o_ref[...] = (jax.nn.sigmoid(acc) * e).astype(o_ref.dtype)
    elif epilogue == "qk":
        g_ref, cos_ref, sin_ref, o_ref, acc_ref = rest
        acc_ref[...] = acc
        lane = lax.broadcasted_iota(jnp.int32, (1, HEAD_DIM), 1)
        first_half = (lane & ROT_HALF) == 0
        gain = g_ref[...]
        cos = cos_ref[...]
        sin = sin_ref[...]
        for h in range(acc_ref.shape[1] // HEAD_DIM):
            sl = slice(h * HEAD_DIM, (h + 1) * HEAD_DIM)
            y = _rms(acc_ref[:, sl], gain)
            partner = jnp.where(first_half,
                                pltpu.roll(y, HEAD_DIM - ROT_HALF, 1),
                                pltpu.roll(y, ROT_HALF, 1))
            o_ref[:, sl] = (y * cos + partner * sin).astype(o_ref.dtype)
    else:
        raise ValueError(epilogue)


def _proj(a, w, col0, ncols, out_dtype, epilogue="none", extra=(), tm=1024, tn=1024):
    m, k = a.shape
    tn = min(tn, ncols)
    assert m % tm == 0 and ncols % tn == 0 and col0 % tn == 0 and w.shape[0] == k
    cb = col0 // tn
    in_specs = [pl.BlockSpec((tm, k), lambda i, j: (i, 0)),
                pl.BlockSpec((k, tn), lambda i, j: (0, cb + j))]
    args = [a, w]
    scratch = []
    extra_bytes = 0
    if epilogue == "qk":
        gain, cos, sin = extra
        in_specs += [pl.BlockSpec((1, HEAD_DIM), lambda i, j: (0, 0)),
                     pl.BlockSpec((tm, HEAD_DIM), lambda i, j: (i, 0)),
                     pl.BlockSpec((tm, HEAD_DIM), lambda i, j: (i, 0))]
        args += [gain.reshape(1, HEAD_DIM), cos, sin]
        scratch = [pltpu.VMEM((tm, tn), F32)]
        extra_bytes = tm * tn * 4 + 4 * tm * HEAD_DIM * 4
    elif epilogue == "ple":
        p, wp = extra
        kp = p.shape[1]
        in_specs += [pl.BlockSpec((tm, kp), lambda i, j: (i, 0)),
                     pl.BlockSpec((kp, tn), lambda i, j: (0, j))]
        args += [p, wp]
        extra_bytes = 2 * (tm * kp + kp * tn) * 2
    out_bytes = jnp.dtype(out_dtype).itemsize
    block_bytes = 2 * (tm * k * 2 + k * tn * 2 + tm * tn * out_bytes) + extra_bytes
    if epilogue == "transpose":
        out_spec = pl.BlockSpec((tn, tm), lambda i, j: (j, i))
        out_shape = jax.ShapeDtypeStruct((ncols, m), out_dtype)
    else:
        out_spec = pl.BlockSpec((tm, tn), lambda i, j: (i, j))
        out_shape = jax.ShapeDtypeStruct((m, ncols), out_dtype)
    return pl.pallas_call(
        functools.partial(_proj_kernel, epilogue=epilogue),
        grid=(m // tm, ncols // tn),
        in_specs=in_specs,
        out_specs=out_spec,
        out_shape=out_shape,
        scratch_shapes=scratch,
        compiler_params=_params(("arbitrary", "arbitrary"), block_bytes),
        name="proj_" + epilogue,
    )(*args)


def _flash_kernel(q_ref, k_ref, vt_ref, *rest, tk, n_cast):
    cast_in = rest[:n_cast]
    o_ref = rest[n_cast]
    cast_out = rest[n_cast + 1:2 * n_cast + 1]
    st_ref, pt_ref, acc_ref = rest[2 * n_cast + 1:]
    for src, dst in zip(cast_in, cast_out):
        dst[...] = src[...].astype(dst.dtype)
    tq = q_ref.shape[0]
    nchunks = k_ref.shape[0] // tk
    qs = [q_ref[:, h * HEAD_DIM:(h + 1) * HEAD_DIM] for h in range(Q_GROUP)]

    def scores(c, s_slot):
        kc = k_ref[pl.ds(pl.multiple_of(c * tk, tk), tk), :]
        for h in range(Q_GROUP):
            st_ref[s_slot, h] = lax.dot_general(kc, qs[h], (((1,), (1,)), ((), ())),
                                                preferred_element_type=F32)

    def softmax_update(ms, ls, s_slot, slot):
        out = []
        nblk = tk // SOFTMAX_ROWS
        for h in range(Q_GROUP):
            blocks = [st_ref.at[s_slot, h, r * SOFTMAX_ROWS:(r + 1) * SOFTMAX_ROWS, :]
                      for r in range(nblk)]
            part = blocks[0][...]
            for blk in blocks[1:]:
                part = jnp.maximum(part, blk[...])
            m_new = jnp.maximum(ms[h], jnp.max(part, axis=0, keepdims=True))
            alpha = jnp.exp2(ms[h] - m_new)
            lpart = None
            for r, blk in enumerate(blocks):
                pt = jnp.exp2(blk[...] - m_new)
                pt_ref[slot, h, r * SOFTMAX_ROWS:(r + 1) * SOFTMAX_ROWS, :] = pt.astype(BF16)
                lpart = pt if lpart is None else lpart + pt
            l_new = alpha * ls[h] + jnp.sum(lpart, axis=0, keepdims=True)
            out.append((m_new, l_new, alpha))
        return tuple(zip(*out))

    def value_update(c, alphas, slot):
        vc = vt_ref[:, pl.ds(pl.multiple_of(c * tk, tk), tk)]
        for h in range(Q_GROUP):
            acc_ref[h] = alphas[h] * acc_ref[h] + jnp.dot(vc, pt_ref[slot, h],
                                                         preferred_element_type=F32)

    def stage(c, k, carry):
        ms, ls, a_old, a_new = carry
        scores(c, k % S_SLOTS)
        value_update(c - 2, a_old, (k - 2) % P_SLOTS)
        ms, ls, a_cur = softmax_update(ms, ls, k % S_SLOTS, k % P_SLOTS)
        return ms, ls, a_new, a_cur

    ms = tuple(jnp.full((1, tq), -jnp.inf, F32) for _ in range(Q_GROUP))
    ls = tuple(jnp.zeros((1, tq), F32) for _ in range(Q_GROUP))
    acc_ref[...] = jnp.zeros(acc_ref.shape, F32)
    scores(0, 0)
    ms, ls, a0 = softmax_update(ms, ls, 0, 0)
    scores(1, 1)
    ms, ls, a1 = softmax_update(ms, ls, 1, 1)
    carry = (ms, ls, a0, a1)

    period = S_SLOTS * P_SLOTS
    ntrips = (nchunks - 2) // period

    def body(t, carry):
        for k in range(2, 2 + period):
            carry = stage(period * t + k, k, carry)
        return carry

    carry = lax.fori_loop(0, ntrips, body, carry)
    for c in range(2 + period * ntrips, nchunks):
        carry = stage(c, c, carry)
    ms, ls, a_old, a_new = carry
    value_update(nchunks - 2, a_old, (nchunks - 2) % P_SLOTS)
    value_update(nchunks - 1, a_new, (nchunks - 1) % P_SLOTS)
    for h in range(Q_GROUP):
        o_ref[:, h * HEAD_DIM:(h + 1) * HEAD_DIM] = (acc_ref[h] / ls[h]).T.astype(o_ref.dtype)


def _cast_block(shape, nsteps):
    rows, cols = shape
    ncol = 1
    while ncol <= nsteps:
        nrow = nsteps // ncol
        if (rows % (nrow * BF16_SUBLANES) == 0 and cols % (ncol * V7X_LANES) == 0
                and nrow * ncol == nsteps):
            return (rows // nrow, cols // ncol), ncol
        ncol *= 2
    raise ValueError(f"cannot tile {shape} over {nsteps} steps")


def _attention(q, k, vt, cast=(), tq=512, tk=512):
    s = q.shape[0]
    gw = Q_GROUP * HEAD_DIM
    nq = s // tq
    nsteps = N_KV_HEADS * nq
    block_bytes = (2 * (tq * gw * 2 + 2 * s * HEAD_DIM * 2 + tq * gw * 4)
                   + Q_GROUP * (tk * tq * (4 * S_SLOTS + 2 * P_SLOTS) + HEAD_DIM * tq * 4))
    cast_in_specs, cast_out_specs = [], []
    for w, layer in cast:
        blk, ncol = _cast_block(w.shape[1:], nsteps)
        cast_in_specs.append(pl.BlockSpec(
            (None,) + blk,
            lambda g, i, ncol=ncol, layer=layer: (layer, (g * nq + i) // ncol, (g * nq + i) % ncol)))
        cast_out_specs.append(pl.BlockSpec(
            blk, lambda g, i, ncol=ncol: ((g * nq + i) // ncol, (g * nq + i) % ncol)))
        block_bytes += 2 * blk[0] * blk[1] * (4 + 2)
    out = pl.pallas_call(
        functools.partial(_flash_kernel, tk=tk, n_cast=len(cast)),
        grid=(N_KV_HEADS, nq),
        in_specs=[pl.BlockSpec((tq, gw), lambda g, i: (i, g)),
                  pl.BlockSpec((s, HEAD_DIM), lambda g, i: (0, g)),
                  pl.BlockSpec((HEAD_DIM, s), lambda g, i: (g, 0))] + cast_in_specs,
        out_specs=[pl.BlockSpec((tq, gw), lambda g, i: (i, g))] + cast_out_specs,
        out_shape=[jax.ShapeDtypeStruct((s, ATTN_WIDTH), F32)]
                  + [jax.ShapeDtypeStruct(w.shape[1:], BF16) for w, _ in cast],
        scratch_shapes=[pltpu.VMEM((S_SLOTS, Q_GROUP, tk, tq), F32),
                        pltpu.VMEM((P_SLOTS, Q_GROUP, tk, tq), BF16),
                        pltpu.VMEM((Q_GROUP, HEAD_DIM, tq), F32)],
        compiler_params=_params(("arbitrary", "arbitrary"), block_bytes),
        name="flash_attention",
    )(q, k, vt, *[w for w, _ in cast])
    return out[0], out[1:]


def _pool_merge_kernel(up_ref, uc_ref, un_ref, wp_ref, ps_ref, gates_ref, attn_ref, o_ref, ext_ref,
                       *, seq):
    i = pl.program_id(0)
    tm = uc_ref.shape[0]
    last = pl.num_programs(0) - 1
    ext_ref[0:POOL_HALO, :] = jnp.where(i > 0, up_ref[...], 0.0)
    ext_ref[POOL_HALO:POOL_HALO + tm, :] = uc_ref[...]
    ext_ref[POOL_HALO + tm:, :] = jnp.where(i < last, un_ref[...], 0.0)
    t = i * tm + lax.broadcasted_iota(jnp.int32, (tm, 1), 0)
    for g in range(N_POOL_GROUPS):
        half = POOL_WINDOWS[g] // 2
        cols = slice(g * POOL_GROUP_IN, (g + 1) * POOL_GROUP_IN)
        wsum = ext_ref[POOL_HALO - half:POOL_HALO - half + tm, cols]
        for d in range(-half + 1, half):
            wsum = wsum + ext_ref[POOL_HALO + d:POOL_HALO + d + tm, cols]
        count = (jnp.minimum(t + half, seq) - jnp.maximum(t - half, 0)).astype(F32)
        delta = wsum / count - uc_ref[:, cols]
        y = jnp.dot(delta.astype(BF16), wp_ref[g], preferred_element_type=F32)
        oc = slice(g * POOL_GROUP_OUT, (g + 1) * POOL_GROUP_OUT)
        pool = y * ps_ref[:, oc]
        ga = gates_ref[:, oc]
        gp = gates_ref[:, D_MODEL + g * POOL_GROUP_OUT:D_MODEL + (g + 1) * POOL_GROUP_OUT]
        o_ref[:, oc] = (ga * attn_ref[:, oc] + gp * pool).astype(o_ref.dtype)


def _pool_merge(u, w_pool, pool_scale, gates, attn, tm=256):
    s = u.shape[0]
    hb = tm // POOL_HALO
    nhb = s // POOL_HALO
    block_bytes = (2 * (tm * POOL_WIDTH * 4 + 2 * POOL_HALO * POOL_WIDTH * 4 + tm * 2 * D_MODEL * 4
                        + tm * D_MODEL * 4 + tm * D_MODEL * 2 + w_pool.size * 2)
                   + (tm + 2 * POOL_HALO) * POOL_WIDTH * 4)
    return pl.pallas_call(
        functools.partial(_pool_merge_kernel, seq=s),
        grid=(s // tm,),
        in_specs=[pl.BlockSpec((POOL_HALO, POOL_WIDTH), lambda i: (jnp.maximum(i * hb - 1, 0), 0)),
                  pl.BlockSpec((tm, POOL_WIDTH), lambda i: (i, 0)),
                  pl.BlockSpec((POOL_HALO, POOL_WIDTH), lambda i: (jnp.minimum((i + 1) * hb, nhb - 1), 0)),
                  pl.BlockSpec(w_pool.shape, lambda i: (0, 0, 0)),
                  pl.BlockSpec((1, D_MODEL), lambda i: (0, 0)),
                  pl.BlockSpec((tm, 2 * D_MODEL), lambda i: (i, 0)),
                  pl.BlockSpec((tm, D_MODEL), lambda i: (i, 0))],
        out_specs=pl.BlockSpec((tm, D_MODEL), lambda i: (i, 0)),
        out_shape=jax.ShapeDtypeStruct((s, D_MODEL), BF16),
        scratch_shapes=[pltpu.VMEM((tm + 2 * POOL_HALO, POOL_WIDTH), F32)],
        compiler_params=_params(("arbitrary",), block_bytes),
        name="pool_merge",
    )(u, u, u, w_pool, pool_scale.reshape(1, D_MODEL), gates, attn)


def _ffn_kernel(h_ref, wg_ref, wu_ref, wd_ref, o_ref):
    @pl.when(pl.program_id(1) == 0)
    def _():
        o_ref[...] = jnp.zeros(o_ref.shape, F32)

    h = h_ref[...]
    g = jnp.dot(h, wg_ref[...], preferred_element_type=F32)
    u = jnp.dot(h, wu_ref[...], preferred_element_type=F32)
    a = (g * jax.nn.sigmoid(g) * u).astype(BF16)
    o_ref[...] += jnp.dot(a, wd_ref[...], preferred_element_type=F32)


def _ffn(h, wg, wu, wd, tm=512, tf=256):
    s, d = h.shape
    dff = wg.shape[1]
    assert dff % tf == 0
    block_bytes = 2 * (tm * d * 2 + 3 * d * tf * 2 + tm * d * 4)
    return pl.pallas_call(
        _ffn_kernel,
        grid=(s // tm, dff // tf),
        in_specs=[pl.BlockSpec((tm, d), lambda i, f: (i, 0)),
                  pl.BlockSpec((d, tf), lambda i, f: (0, f)),
                  pl.BlockSpec((d, tf), lambda i, f: (0, f)),
                  pl.BlockSpec((tf, d), lambda i, f: (f, 0))],
        out_specs=pl.BlockSpec((tm, d), lambda i, f: (i, 0)),
        out_shape=jax.ShapeDtypeStruct((s, d), F32),
        compiler_params=_params(("arbitrary", "arbitrary"), block_bytes),
        name="swiglu_ffn",
    )(h, wg, wu, wd)


def _rope_tables(seq):
    rows = seq // GRID_W
    row = jnp.broadcast_to(jnp.arange(rows)[:, None], (rows, GRID_W)).reshape(seq)
    col = jnp.broadcast_to(jnp.arange(GRID_W)[None, :], (rows, GRID_W)).reshape(seq)
    inv_freq = ROPE_THETA ** (-jnp.arange(0, AXIS_ROT_DIM, 2, dtype=F32) / AXIS_ROT_DIM)
    pos = jnp.stack([row, col], axis=-1).astype(F32)
    ang = pos[:, :, None] * inv_freq[None, None, :]
    cos, sin = jnp.cos(ang), jnp.sin(ang)
    cos_t = jnp.concatenate([cos, cos], axis=-1).reshape(seq, HEAD_DIM)
    sin_t = jnp.concatenate([-sin, sin], axis=-1).reshape(seq, HEAD_DIM)
    return cos_t, sin_t


def kernel(x, p, w_in, q_norm, k_norm, w_pool, pool_scale, w_out, norm_mix_pre, norm_mix_post,
           norm_ffn_pre, norm_ffn_post, w_ffn_gate, w_ffn_up, w_ffn_down, w_ple_in, w_ple_gate,
           norm_ple):
    b, s, d = x.shape
    assert b == 1 and d == D_MODEL and s % GRID_W == 0
    cos_t, sin_t = _rope_tables(s)
    xs = x.reshape(s, d)
    h = _norm_cast(xs, norm_mix_pre[0])
    w_in_b = w_in[0].astype(BF16)
    for i in range(DEPTH):
        q = _proj(h, w_in_b, COL_Q, ATTN_WIDTH, BF16, "qk",
                  (q_norm[i] * (HEAD_DIM ** -0.5 * LOG2_E), cos_t, sin_t))
        k = _proj(h, w_in_b, COL_K, KV_WIDTH, BF16, "qk", (k_norm[i], cos_t, sin_t))
        vt = _proj(h, w_in_b, COL_V, KV_WIDTH, BF16, "transpose")
        u = _proj(h, w_in_b, COL_U, POOL_WIDTH, F32)
        gates = _proj(h, w_in_b, COL_GATES, 2 * D_MODEL, F32, "sigmoid")
        later = [(w_out, i), (w_ffn_gate, i), (w_ffn_up, i), (w_ffn_down, i), (w_ple_gate, i)]
        if i + 1 < DEPTH:
            later.append((w_in, i + 1))
        attn, later_b = _attention(q, k, vt, later)
        w_out_b, w_gate_b, w_up_b, w_down_b, w_ple_gate_b = later_b[:5]
        if i + 1 < DEPTH:
            w_in_b = later_b[5]
        merged = _pool_merge(u, w_pool[i].astype(BF16), pool_scale[i], gates, attn)
        o = _proj(merged, w_out_b, 0, D_MODEL, F32)
        xs, h = _resnorm(xs, o, norm_mix_post[i], norm_ffn_pre[i], "norm")
        f = _ffn(h, w_gate_b, w_up_b, w_down_b)
        xs, xb = _resnorm(xs, f, norm_ffn_post[i], None, "cast")
        ge = _proj(xb, w_ple_gate_b, 0, D_MODEL, F32, "ple",
                   (p[i].reshape(s, D_PLE).astype(BF16), w_ple_in[i].astype(BF16)))
        if i + 1 < DEPTH:
            xs, h = _resnorm(xs, ge, norm_ple[i], norm_mix_pre[i + 1], "norm")
        else:
            xs, _ = _resnorm(xs, ge, norm_ple[i], None, "none")
    return xs.reshape(b, s, d)
```

```python
import functools

import jax
import jax.numpy as jnp
from jax import lax
from jax.experimental import pallas as pl
from jax.experimental.pallas import tpu as pltpu

D_MODEL = 4096
SEQ = 8192
DEPTH = 2
N_HEADS = 32
N_KV_HEADS = 8
HEAD_DIM = 128
Q_GROUP = N_HEADS // N_KV_HEADS
ATTN_WIDTH = N_HEADS * HEAD_DIM
KV_WIDTH = N_KV_HEADS * HEAD_DIM
AXIS_ROT_DIM = HEAD_DIM // 2
ROT_HALF = AXIS_ROT_DIM // 2
ROPE_THETA = 10000.0
GRID_W = 64
POOL_WINDOWS = (2, 4, 8, 16)
N_POOL_GROUPS = 4
POOL_WIDTH = 2048
POOL_GROUP_IN = POOL_WIDTH // N_POOL_GROUPS
POOL_GROUP_OUT = D_MODEL // N_POOL_GROUPS
D_FF = 11008
D_PLE = 256
EPS = 1e-6
LOG2_E = 1.4426950408889634

COL_Q = 0
COL_K = COL_Q + ATTN_WIDTH
COL_V = COL_K + KV_WIDTH
COL_U = COL_V + KV_WIDTH
COL_GATES = COL_U + POOL_WIDTH

V7X_VMEM_BYTES = 64 * 1024 * 1024
V7X_SUBLANES = 8
V7X_LANES = 128
BF16_SUBLANES = 2 * V7X_SUBLANES
POOL_HALO = 8
SOFTMAX_ROWS = 32
QK_EPILOGUE_ROWS = 128
S_SLOTS = 2
P_SLOTS = 3
VMEM_TEMP_BYTES = 12 * 1024 * 1024

F32 = jnp.float32
BF16 = jnp.bfloat16


def _params(semantics, block_bytes):
    limit = min(int(block_bytes) + VMEM_TEMP_BYTES, V7X_VMEM_BYTES - 4 * 1024 * 1024)
    return pltpu.CompilerParams(dimension_semantics=semantics, vmem_limit_bytes=limit)


def _rms(x, gain):
    ms = jnp.mean(x * x, axis=-1, keepdims=True)
    return x * lax.rsqrt(ms + EPS) * gain


def _norm_cast_kernel(x_ref, g_ref, o_ref):
    o_ref[...] = _rms(x_ref[...], g_ref[...]).astype(o_ref.dtype)


def _norm_cast(x, gain, tm=256):
    s, d = x.shape
    return pl.pallas_call(
        _norm_cast_kernel,
        grid=(s // tm,),
        in_specs=[pl.BlockSpec((tm, d), lambda i: (i, 0)),
                  pl.BlockSpec((1, d), lambda i: (0, 0))],
        out_specs=pl.BlockSpec((tm, d), lambda i: (i, 0)),
        out_shape=jax.ShapeDtypeStruct((s, d), BF16),
        compiler_params=_params(("arbitrary",), 2 * tm * d * 6),
        name="norm_cast",
    )(x, gain.reshape(1, d))


def _resnorm_kernel(x_ref, f_ref, gpost_ref, *rest, mode):
    xn = x_ref[...] + _rms(f_ref[...], gpost_ref[...])
    if mode == "norm":
        gnext_ref, xo_ref, ho_ref = rest
        ho_ref[...] = _rms(xn, gnext_ref[...]).astype(ho_ref.dtype)
    elif mode == "cast":
        xo_ref, ho_ref = rest
        ho_ref[...] = xn.astype(ho_ref.dtype)
    else:
        (xo_ref,) = rest
    xo_ref[...] = xn


def _resnorm(x, f, gpost, gnext=None, mode="none", tm=256):
    s, d = x.shape
    row = pl.BlockSpec((tm, d), lambda i: (i, 0))
    vec = pl.BlockSpec((1, d), lambda i: (0, 0))
    in_specs = [row, row, vec]
    args = [x, f, gpost.reshape(1, d)]
    if mode == "norm":
        in_specs.append(vec)
        args.append(gnext.reshape(1, d))
    out_shape = [jax.ShapeDtypeStruct((s, d), F32)]
    out_specs = [row]
    if mode != "none":
        out_shape.append(jax.ShapeDtypeStruct((s, d), BF16))
        out_specs.append(row)
    out = pl.pallas_call(
        functools.partial(_resnorm_kernel, mode=mode),
        grid=(s // tm,),
        in_specs=in_specs,
        out_specs=out_specs,
        out_shape=out_shape,
        compiler_params=_params(("arbitrary",), 2 * tm * d * 14),
        name="resnorm_" + mode,
    )(*args)
    return out if mode != "none" else (out[0], None)


def _proj_kernel(a_ref, w_ref, *rest, epilogue):
    acc = jnp.dot(a_ref[...], w_ref[...], preferred_element_type=F32)
    if epilogue == "none":
        (o_ref,) = rest
        o_ref[...] = acc.astype(o_ref.dtype)
    elif epilogue == "transpose":
        (o_ref,) = rest
        o_ref[...] = acc.T.astype(o_ref.dtype)
    elif epilogue == "sigmoid":
        (o_ref,) = rest
        o_ref[...] = jax.nn.sigmoid(acc).astype(o_ref.dtype)
    elif epilogue == "ple":
        p_ref, wp_ref, o_ref = rest
        e = jnp.dot(p_ref[...], wp_ref[...], preferred_element_type=F32)
        o_ref[...] = (jax.nn.sigmoid(acc) * e).astype(o_ref.dtype)
    elif epilogue == "qk":
        g_ref, cos_ref, sin_ref, o_ref, acc_ref = rest
        acc_ref[...] = acc
        lane = lax.broadcasted_iota(jnp.int32, (1, HEAD_DIM), 1)
        first_half = (lane & ROT_HALF) == 0
        gain = g_ref[...]
        for r in range(acc_ref.shape[0] // QK_EPILOGUE_ROWS):
            rows = slice(r * QK_EPILOGUE_ROWS, (r + 1) * QK_EPILOGUE_ROWS)
            cos = cos_ref[rows, :]
            sin = sin_ref[rows, :]
            for h in range(acc_ref.shape[1] // HEAD_DIM):
                sl = slice(h * HEAD_DIM, (h + 1) * HEAD_DIM)
                y = _rms(acc_ref[rows, sl], gain)
                partner = jnp.where(first_half,
                                    pltpu.roll(y, HEAD_DIM - ROT_HALF, 1),
                                    pltpu.roll(y, ROT_HALF, 1))
                o_ref[rows, sl] = (y * cos + partner * sin).astype(o_ref.dtype)
    else:
        raise ValueError(epilogue)


def _proj(a, w, col0, ncols, out_dtype, epilogue="none", extra=(), tm=1024, tn=1024):
    m, k = a.shape
    tn = min(tn, ncols)
    assert m % tm == 0 and ncols % tn == 0 and col0 % tn == 0 and w.shape[0] == k
    cb = col0 // tn
    in_specs = [pl.BlockSpec((tm, k), lambda i, j: (i, 0)),
                pl.BlockSpec((k, tn), lambda i, j: (0, cb + j))]
    args = [a, w]
    scratch = []
    extra_bytes = 0
    if epilogue == "qk":
        gain, cos, sin = extra
        in_specs += [pl.BlockSpec((1, HEAD_DIM), lambda i, j: (0, 0)),
                     pl.BlockSpec((tm, HEAD_DIM), lambda i, j: (i, 0)),
                     pl.BlockSpec((tm, HEAD_DIM), lambda i, j: (i, 0))]
        args += [gain.reshape(1, HEAD_DIM), cos, sin]
        scratch = [pltpu.VMEM((tm, tn), F32)]
        extra_bytes = tm * tn * 4 + 4 * tm * HEAD_DIM * 4
    elif epilogue == "ple":
        p, wp = extra
        kp = p.shape[1]
        in_specs += [pl.BlockSpec((tm, kp), lambda i, j: (i, 0)),
                     pl.BlockSpec((kp, tn), lambda i, j: (0, j))]
        args += [p, wp]
        extra_bytes = 2 * (tm * kp + kp * tn) * 2
    out_bytes = jnp.dtype(out_dtype).itemsize
    block_bytes = 2 * (tm * k * 2 + k * tn * 2 + tm * tn * out_bytes) + extra_bytes
    if epilogue == "transpose":
        out_spec = pl.BlockSpec((tn, tm), lambda i, j: (j, i))
        out_shape = jax.ShapeDtypeStruct((ncols, m), out_dtype)
    else:
        out_spec = pl.BlockSpec((tm, tn), lambda i, j: (i, j))
        out_shape = jax.ShapeDtypeStruct((m, ncols), out_dtype)
    return pl.pallas_call(
        functools.partial(_proj_kernel, epilogue=epilogue),
        grid=(m // tm, ncols // tn),
        in_specs=in_specs,
        out_specs=out_spec,
        out_shape=out_shape,
        scratch_shapes=scratch,
        compiler_params=_params(("arbitrary", "arbitrary"), block_bytes),
        name="proj_" + epilogue,
    )(*args)


def _flash_kernel(q_ref, k_ref, vt_ref, *rest, tk, n_cast):
    cast_in = rest[:n_cast]
    o_ref = rest[n_cast]
    cast_out = rest[n_cast + 1:2 * n_cast + 1]
    st_ref, pt_ref, acc_ref = rest[2 * n_cast + 1:]
    for src, dst in zip(cast_in, cast_out):
        dst[...] = src[...].astype(dst.dtype)
    tq = q_ref.shape[0]
    nchunks = k_ref.shape[0] // tk
    qs = [q_ref[:, h * HEAD_DIM:(h + 1) * HEAD_DIM] for h in range(Q_GROUP)]

    def scores(c, s_slot):
        kc = k_ref[pl.ds(pl.multiple_of(c * tk, tk), tk), :]
        cms = []
        for h in range(Q_GROUP):
            st = lax.dot_general(kc, qs[h], (((1,), (1,)), ((), ())), preferred_element_type=F32)
            st_ref[s_slot, h] = st
            cms.append(jnp.max(st, axis=0, keepdims=True))
        return cms

    def softmax_update(ms, cms, s_slot, slot):
        out = []
        for h in range(Q_GROUP):
            m_new = jnp.maximum(ms[h], cms[h])
            alpha = jnp.exp2(ms[h] - m_new)
            for r in range(tk // SOFTMAX_ROWS):
                rows = slice(r * SOFTMAX_ROWS, (r + 1) * SOFTMAX_ROWS)
                pt_ref[slot, h, rows, :] = jnp.exp2(st_ref[s_slot, h, rows, :] - m_new).astype(BF16)
            out.append((m_new, alpha))
        return tuple(zip(*out))

    ones_rows = (lax.broadcasted_iota(jnp.int32, (BF16_SUBLANES, tk), 0) == 0).astype(BF16)

    def value_update(c, alphas, slot):
        vc = vt_ref[:, pl.ds(pl.multiple_of(c * tk, tk), tk)]
        vc = jnp.concatenate([vc, ones_rows], axis=0)
        for h in range(Q_GROUP):
            acc_ref[h] = alphas[h] * acc_ref[h] + jnp.dot(vc, pt_ref[slot, h],
                                                         preferred_element_type=F32)

    def stage(c, k, carry):
        ms, a_old, a_new = carry
        cms = scores(c, k % S_SLOTS)
        value_update(c - 2, a_old, (k - 2) % P_SLOTS)
        ms, a_cur = softmax_update(ms, cms, k % S_SLOTS, k % P_SLOTS)
        return ms, a_new, a_cur

    ms = tuple(jnp.full((1, tq), -jnp.inf, F32) for _ in range(Q_GROUP))
    acc_ref[...] = jnp.zeros(acc_ref.shape, F32)
    ms, a0 = softmax_update(ms, scores(0, 0), 0, 0)
    ms, a1 = softmax_update(ms, scores(1, 1), 1, 1)
    carry = (ms, a0, a1)

    period = S_SLOTS * P_SLOTS
    ntrips = (nchunks - 2) // period

    def body(t, carry):
        for k in range(2, 2 + period):
            carry = stage(period * t + k, k, carry)
        return carry

    carry = lax.fori_loop(0, ntrips, body, carry)
    for c in range(2 + period * ntrips, nchunks):
        carry = stage(c, c, carry)
    ms, a_old, a_new = carry
    value_update(nchunks - 2, a_old, (nchunks - 2) % P_SLOTS)
    value_update(nchunks - 1, a_new, (nchunks - 1) % P_SLOTS)
    for h in range(Q_GROUP):
        out_t = acc_ref[h, 0:HEAD_DIM, :] / acc_ref[h, HEAD_DIM:HEAD_DIM + 1, :]
        o_ref[:, h * HEAD_DIM:(h + 1) * HEAD_DIM] = out_t.T.astype(o_ref.dtype)


def _cast_block(shape, nsteps):
    rows, cols = shape
    ncol = 1
    while ncol <= nsteps:
        nrow = nsteps // ncol
        if (rows % (nrow * BF16_SUBLANES) == 0 and cols % (ncol * V7X_LANES) == 0
                and nrow * ncol == nsteps):
            return (rows // nrow, cols // ncol), ncol
        ncol *= 2
    raise ValueError(f"cannot tile {shape} over {nsteps} steps")


def _attention(q, k, vt, cast=(), tq=512, tk=512):
    s = q.shape[0]
    gw = Q_GROUP * HEAD_DIM
    nq = s // tq
    nsteps = N_KV_HEADS * nq
    block_bytes = (2 * (tq * gw * 2 + 2 * s * HEAD_DIM * 2 + tq * gw * 2)
                   + Q_GROUP * (tk * tq * (4 * S_SLOTS + 2 * P_SLOTS) + HEAD_DIM * tq * 4))
    cast_in_specs, cast_out_specs = [], []
    for w, layer in cast:
        blk, ncol = _cast_block(w.shape[1:], nsteps)
        cast_in_specs.append(pl.BlockSpec(
            (None,) + blk,
            lambda g, i, ncol=ncol, layer=layer: (layer, (g * nq + i) // ncol, (g * nq + i) % ncol)))
        cast_out_specs.append(pl.BlockSpec(
            blk, lambda g, i, ncol=ncol: ((g * nq + i) // ncol, (g * nq + i) % ncol)))
        block_bytes += 2 * blk[0] * blk[1] * (4 + 2)
    out = pl.pallas_call(
        functools.partial(_flash_kernel, tk=tk, n_cast=len(cast)),
        grid=(N_KV_HEADS, nq),
        in_specs=[pl.BlockSpec((tq, gw), lambda g, i: (i, g)),
                  pl.BlockSpec((s, HEAD_DIM), lambda g, i: (0, g)),
                  pl.BlockSpec((HEAD_DIM, s), lambda g, i: (g, 0))] + cast_in_specs,
        out_specs=[pl.BlockSpec((tq, gw), lambda g, i: (i, g))] + cast_out_specs,
        out_shape=[jax.ShapeDtypeStruct((s, ATTN_WIDTH), BF16)]
                  + [jax.ShapeDtypeStruct(w.shape[1:], BF16) for w, _ in cast],
        scratch_shapes=[pltpu.VMEM((S_SLOTS, Q_GROUP, tk, tq), F32),
                        pltpu.VMEM((P_SLOTS, Q_GROUP, tk, tq), BF16),
                        pltpu.VMEM((Q_GROUP, HEAD_DIM + BF16_SUBLANES, tq), F32)],
        compiler_params=_params(("arbitrary", "arbitrary"), block_bytes),
        name="flash_attention",
    )(q, k, vt, *[w for w, _ in cast])
    return out[0], out[1:]


def _pool_merge_kernel(up_ref, uc_ref, un_ref, wp_ref, ps_ref, gates_ref, attn_ref, o_ref, ext_ref,
                       *, seq):
    i = pl.program_id(0)
    tm = uc_ref.shape[0]
    last = pl.num_programs(0) - 1
    ext_ref[0:POOL_HALO, :] = jnp.where(i > 0, up_ref[...], 0.0)
    ext_ref[POOL_HALO:POOL_HALO + tm, :] = uc_ref[...]
    ext_ref[POOL_HALO + tm:, :] = jnp.where(i < last, un_ref[...], 0.0)
    t = i * tm + lax.broadcasted_iota(jnp.int32, (tm, 1), 0)
    for g in range(N_POOL_GROUPS):
        half = POOL_WINDOWS[g] // 2
        cols = slice(g * POOL_GROUP_IN, (g + 1) * POOL_GROUP_IN)
        wsum = ext_ref[POOL_HALO - half:POOL_HALO - half + tm, cols]
        for d in range(-half + 1, half):
            wsum = wsum + ext_ref[POOL_HALO + d:POOL_HALO + d + tm, cols]
        count = (jnp.minimum(t + half, seq) - jnp.maximum(t - half, 0)).astype(F32)
        delta = wsum / count - uc_ref[:, cols]
        y = jnp.dot(delta.astype(BF16), wp_ref[g], preferred_element_type=F32)
        oc = slice(g * POOL_GROUP_OUT, (g + 1) * POOL_GROUP_OUT)
        pool = y * ps_ref[:, oc]
        ga = gates_ref[:, oc].astype(F32)
        gp = gates_ref[:, D_MODEL + g * POOL_GROUP_OUT:D_MODEL + (g + 1) * POOL_GROUP_OUT].astype(F32)
        o_ref[:, oc] = (ga * attn_ref[:, oc].astype(F32) + gp * pool).astype(o_ref.dtype)


def _pool_merge(u, w_pool, pool_scale, gates, attn, tm=256):
    s = u.shape[0]
    hb = tm // POOL_HALO
    nhb = s // POOL_HALO
    block_bytes = (2 * (tm * POOL_WIDTH * 4 + 2 * POOL_HALO * POOL_WIDTH * 4
                        + tm * 2 * D_MODEL * gates.dtype.itemsize
                        + tm * D_MODEL * attn.dtype.itemsize + tm * D_MODEL * 2 + w_pool.size * 2)
                   + (tm + 2 * POOL_HALO) * POOL_WIDTH * 4)
    return pl.pallas_call(
        functools.partial(_pool_merge_kernel, seq=s),
        grid=(s // tm,),
        in_specs=[pl.BlockSpec((POOL_HALO, POOL_WIDTH), lambda i: (jnp.maximum(i * hb - 1, 0), 0)),
                  pl.BlockSpec((tm, POOL_WIDTH), lambda i: (i, 0)),
                  pl.BlockSpec((POOL_HALO, POOL_WIDTH), lambda i: (jnp.minimum((i + 1) * hb, nhb - 1), 0)),
                  pl.BlockSpec(w_pool.shape, lambda i: (0, 0, 0)),
                  pl.BlockSpec((1, D_MODEL), lambda i: (0, 0)),
                  pl.BlockSpec((tm, 2 * D_MODEL), lambda i: (i, 0)),
                  pl.BlockSpec((tm, D_MODEL), lambda i: (i, 0))],
        out_specs=pl.BlockSpec((tm, D_MODEL), lambda i: (i, 0)),
        out_shape=jax.ShapeDtypeStruct((s, D_MODEL), BF16),
        scratch_shapes=[pltpu.VMEM((tm + 2 * POOL_HALO, POOL_WIDTH), F32)],
        compiler_params=_params(("arbitrary",), block_bytes),
        name="pool_merge",
    )(u, u, u, w_pool, pool_scale.reshape(1, D_MODEL), gates, attn)


def _ffn_kernel(h_ref, wg_ref, wu_ref, wd_ref, o_ref):
    @pl.when(pl.program_id(1) == 0)
    def _():
        o_ref[...] = jnp.zeros(o_ref.shape, F32)

    h = h_ref[...]
    g = jnp.dot(h, wg_ref[...], preferred_element_type=F32)
    u = jnp.dot(h, wu_ref[...], preferred_element_type=F32)
    a = (g * jax.nn.sigmoid(g) * u).astype(BF16)
    o_ref[...] += jnp.dot(a, wd_ref[...], preferred_element_type=F32)


def _ffn(h, wg, wu, wd, tm=512, tf=256):
    s, d = h.shape
    dff = wg.shape[1]
    assert dff % tf == 0
    block_bytes = 2 * (tm * d * 2 + 3 * d * tf * 2 + tm * d * 4)
    return pl.pallas_call(
        _ffn_kernel,
        grid=(s // tm, dff // tf),
        in_specs=[pl.BlockSpec((tm, d), lambda i, f: (i, 0)),
                  pl.BlockSpec((d, tf), lambda i, f: (0, f)),
                  pl.BlockSpec((d, tf), lambda i, f: (0, f)),
                  pl.BlockSpec((tf, d), lambda i, f: (f, 0))],
        out_specs=pl.BlockSpec((tm, d), lambda i, f: (i, 0)),
        out_shape=jax.ShapeDtypeStruct((s, d), F32),
        compiler_params=_params(("arbitrary", "arbitrary"), block_bytes),
        name="swiglu_ffn",
    )(h, wg, wu, wd)


def _rope_tables(seq):
    rows = seq // GRID_W
    row = jnp.broadcast_to(jnp.arange(rows)[:, None], (rows, GRID_W)).reshape(seq)
    col = jnp.broadcast_to(jnp.arange(GRID_W)[None, :], (rows, GRID_W)).reshape(seq)
    inv_freq = ROPE_THETA ** (-jnp.arange(0, AXIS_ROT_DIM, 2, dtype=F32) / AXIS_ROT_DIM)
    pos = jnp.stack([row, col], axis=-1).astype(F32)
    ang = pos[:, :, None] * inv_freq[None, None, :]
    cos, sin = jnp.cos(ang), jnp.sin(ang)
    cos_t = jnp.concatenate([cos, cos], axis=-1).reshape(seq, HEAD_DIM)
    sin_t = jnp.concatenate([-sin, sin], axis=-1).reshape(seq, HEAD_DIM)
    return cos_t, sin_t


def kernel(x, p, w_in, q_norm, k_norm, w_pool, pool_scale, w_out, norm_mix_pre, norm_mix_post,
           norm_ffn_pre, norm_ffn_post, w_ffn_gate, w_ffn_up, w_ffn_down, w_ple_in, w_ple_gate,
           norm_ple):
    b, s, d = x.shape
    assert b == 1 and d == D_MODEL and s % GRID_W == 0
    cos_t, sin_t = _rope_tables(s)
    xs = x.reshape(s, d)
    h = _norm_cast(xs, norm_mix_pre[0])
    w_in_b = w_in[0].astype(BF16)
    for i in range(DEPTH):
        q = _proj(h, w_in_b, COL_Q, ATTN_WIDTH, BF16, "qk",
                  (q_norm[i] * (HEAD_DIM ** -0.5 * LOG2_E), cos_t, sin_t))
        k = _proj(h, w_in_b, COL_K, KV_WIDTH, BF16, "qk", (k_norm[i], cos_t, sin_t))
        vt = _proj(h, w_in_b, COL_V, KV_WIDTH, BF16, "transpose")
        u = _proj(h, w_in_b, COL_U, POOL_WIDTH, F32)
        gates = _proj(h, w_in_b, COL_GATES, 2 * D_MODEL, BF16, "sigmoid")
        later = [(w_out, i), (w_ffn_gate, i), (w_ffn_up, i), (w_ffn_down, i), (w_ple_gate, i)]
        if i + 1 < DEPTH:
            later.append((w_in, i + 1))
        attn, later_b = _attention(q, k, vt, later)
        w_out_b, w_gate_b, w_up_b, w_down_b, w_ple_gate_b = later_b[:5]
        if i + 1 < DEPTH:
            w_in_b = later_b[5]
        merged = _pool_merge(u, w_pool[i].astype(BF16), pool_scale[i], gates, attn)
        o = _proj(merged, w_out_b, 0, D_MODEL, F32)
        xs, h = _resnorm(xs, o, norm_mix_post[i], norm_ffn_pre[i], "norm")
        f = _ffn(h, w_gate_b, w_up_b, w_down_b)
        xs, xb = _resnorm(xs, f, norm_ffn_post[i], None, "cast")
        ge = _proj(xb, w_ple_gate_b, 0, D_MODEL, F32, "ple",
                   (p[i].reshape(s, D_PLE).astype(BF16), w_ple_in[i].astype(BF16)))
        if i + 1 < DEPTH:
            xs, h = _resnorm(xs, ge, norm_ple[i], norm_mix_pre[i + 1], "norm")
        else:
            xs, _ = _resnorm(xs, ge, norm_ple[i], None, "none")
    return xs.reshape(b, s, d)
```

```python
import functools

import jax
import jax.numpy as jnp
from jax import lax
from jax.experimental import pallas as pl
from jax.experimental.pallas import tpu as pltpu

D_MODEL = 4096
SEQ = 8192
DEPTH = 2
N_HEADS = 32
N_KV_HEADS = 8
HEAD_DIM = 128
Q_GROUP = N_HEADS // N_KV_HEADS
ATTN_WIDTH = N_HEADS * HEAD_DIM
KV_WIDTH = N_KV_HEADS * HEAD_DIM
AXIS_ROT_DIM = HEAD_DIM // 2
ROT_HALF = AXIS_ROT_DIM // 2
ROPE_THETA = 10000.0
GRID_W = 64
POOL_WINDOWS = (2, 4, 8, 16)
N_POOL_GROUPS = 4
POOL_WIDTH = 2048
POOL_GROUP_IN = POOL_WIDTH // N_POOL_GROUPS
POOL_GROUP_OUT = D_MODEL // N_POOL_GROUPS
D_FF = 11008
D_PLE = 256
EPS = 1e-6
LOG2_E = 1.4426950408889634

COL_Q = 0
COL_K = COL_Q + ATTN_WIDTH
COL_V = COL_K + KV_WIDTH
COL_U = COL_V + KV_WIDTH
COL_GATES = COL_U + POOL_WIDTH

V7X_VMEM_BYTES = 64 * 1024 * 1024
V7X_SUBLANES = 8
V7X_LANES = 128
BF16_SUBLANES = 2 * V7X_SUBLANES
POOL_HALO = 8
SOFTMAX_ROWS = 32
QK_EPILOGUE_ROWS = 128
MAX_LAGGED_EXPONENT = 64.0
P_SLOTS = 3
VMEM_TEMP_BYTES = 12 * 1024 * 1024

F32 = jnp.float32
BF16 = jnp.bfloat16


def _params(semantics, block_bytes):
    limit = min(int(block_bytes) + VMEM_TEMP_BYTES, V7X_VMEM_BYTES - 4 * 1024 * 1024)
    return pltpu.CompilerParams(dimension_semantics=semantics, vmem_limit_bytes=limit)


def _rms(x, gain):
    ms = jnp.mean(x * x, axis=-1, keepdims=True)
    return x * lax.rsqrt(ms + EPS) * gain


def _norm_cast_kernel(x_ref, g_ref, o_ref):
    o_ref[...] = _rms(x_ref[...], g_ref[...]).astype(o_ref.dtype)


def _norm_cast(x, gain, tm=256):
    s, d = x.shape
    return pl.pallas_call(
        _norm_cast_kernel,
        grid=(s // tm,),
        in_specs=[pl.BlockSpec((tm, d), lambda i: (i, 0)),
                  pl.BlockSpec((1, d), lambda i: (0, 0))],
        out_specs=pl.BlockSpec((tm, d), lambda i: (i, 0)),
        out_shape=jax.ShapeDtypeStruct((s, d), BF16),
        compiler_params=_params(("arbitrary",), 2 * tm * d * 6),
        name="norm_cast",
    )(x, gain.reshape(1, d))


def _resnorm_kernel(x_ref, f_ref, gpost_ref, *rest, mode):
    xn = x_ref[...] + _rms(f_ref[...], gpost_ref[...])
    if mode == "norm":
        gnext_ref, xo_ref, ho_ref = rest
        ho_ref[...] = _rms(xn, gnext_ref[...]).astype(ho_ref.dtype)
    elif mode == "cast":
        xo_ref, ho_ref = rest
        ho_ref[...] = xn.astype(ho_ref.dtype)
    else:
        (xo_ref,) = rest
    xo_ref[...] = xn


def _resnorm(x, f, gpost, gnext=None, mode="none", tm=256):
    s, d = x.shape
    row = pl.BlockSpec((tm, d), lambda i: (i, 0))
    vec = pl.BlockSpec((1, d), lambda i: (0, 0))
    in_specs = [row, row, vec]
    args = [x, f, gpost.reshape(1, d)]
    if mode == "norm":
        in_specs.append(vec)
        args.append(gnext.reshape(1, d))
    out_shape = [jax.ShapeDtypeStruct((s, d), F32)]
    out_specs = [row]
    if mode != "none":
        out_shape.append(jax.ShapeDtypeStruct((s, d), BF16))
        out_specs.append(row)
    out = pl.pallas_call(
        functools.partial(_resnorm_kernel, mode=mode),
        grid=(s // tm,),
        in_specs=in_specs,
        out_specs=out_specs,
        out_shape=out_shape,
        compiler_params=_params(("arbitrary",), 2 * tm * d * 14),
        name="resnorm_" + mode,
    )(*args)
    return out if mode != "none" else (out[0], None)


def _proj_kernel(a_ref, w_ref, *rest, epilogue):
    acc = jnp.dot(a_ref[...], w_ref[...], preferred_element_type=F32)
    if epilogue == "none":
        (o_ref,) = rest
        o_ref[...] = acc.astype(o_ref.dtype)
    elif epilogue == "transpose":
        (o_ref,) = rest
        o_ref[...] = acc.T.astype(o_ref.dtype)
    elif epilogue == "sigmoid":
        (o_ref,) = rest
        o_ref[...] = jax.nn.sigmoid(acc).astype(o_ref.dtype)
    elif epilogue == "ple":
        p_ref, wp_ref, o_ref = rest
        e = jnp.dot(p_ref[...], wp_ref[...], preferred_element_type=F32)
        o_ref[...] = (jax.nn.sigmoid(acc) * e).astype(o_ref.dtype)
    elif epilogue == "qk":
        g_ref, cos_ref, sin_ref, o_ref, acc_ref = rest
        acc_ref[...] = acc
        lane = lax.broadcasted_iota(jnp.int32, (1, HEAD_DIM), 1)
        first_half = (lane & ROT_HALF) == 0
        gain = g_ref[...]
        for r in range(acc_ref.shape[0] // QK_EPILOGUE_ROWS):
            rows = slice(r * QK_EPILOGUE_ROWS, (r + 1) * QK_EPILOGUE_ROWS)
            cos = cos_ref[rows, :]
            sin = sin_ref[rows, :]
            for h in range(acc_ref.shape[1] // HEAD_DIM):
                sl = slice(h * HEAD_DIM, (h + 1) * HEAD_DIM)
                y = _rms(acc_ref[rows, sl], gain)
                partner = jnp.where(first_half,
                                    pltpu.roll(y, HEAD_DIM - ROT_HALF, 1),
                                    pltpu.roll(y, ROT_HALF, 1))
                o_ref[rows, sl] = (y * cos + partner * sin).astype(o_ref.dtype)
    else:
        raise ValueError(epilogue)


def _proj(a, w, col0, ncols, out_dtype, epilogue="none", extra=(), tm=1024, tn=1024):
    m, k = a.shape
    tn = min(tn, ncols)
    assert m % tm == 0 and ncols % tn == 0 and col0 % tn == 0 and w.shape[0] == k
    cb = col0 // tn
    in_specs = [pl.BlockSpec((tm, k), lambda i, j: (i, 0)),
                pl.BlockSpec((k, tn), lambda i, j: (0, cb + j))]
    args = [a, w]
    scratch = []
    extra_bytes = 0
    if epilogue == "qk":
        gain, cos, sin = extra
        in_specs += [pl.BlockSpec((1, HEAD_DIM), lambda i, j: (0, 0)),
                     pl.BlockSpec((tm, HEAD_DIM), lambda i, j: (i, 0)),
                     pl.BlockSpec((tm, HEAD_DIM), lambda i, j: (i, 0))]
        args += [gain.reshape(1, HEAD_DIM), cos, sin]
        scratch = [pltpu.VMEM((tm, tn), F32)]
        extra_bytes = tm * tn * 4 + 4 * tm * HEAD_DIM * 4
    elif epilogue == "ple":
        p, wp = extra
        kp = p.shape[1]
        in_specs += [pl.BlockSpec((tm, kp), lambda i, j: (i, 0)),
                     pl.BlockSpec((kp, tn), lambda i, j: (0, j))]
        args += [p, wp]
        extra_bytes = 2 * (tm * kp + kp * tn) * 2
    out_bytes = jnp.dtype(out_dtype).itemsize
    block_bytes = 2 * (tm * k * 2 + k * tn * 2 + tm * tn * out_bytes) + extra_bytes
    if epilogue == "transpose":
        out_spec = pl.BlockSpec((tn, tm), lambda i, j: (j, i))
        out_shape = jax.ShapeDtypeStruct((ncols, m), out_dtype)
    else:
        out_spec = pl.BlockSpec((tm, tn), lambda i, j: (i, j))
        out_shape = jax.ShapeDtypeStruct((m, ncols), out_dtype)
    return pl.pallas_call(
        functools.partial(_proj_kernel, epilogue=epilogue),
        grid=(m // tm, ncols // tn),
        in_specs=in_specs,
        out_specs=out_spec,
        out_shape=out_shape,
        scratch_shapes=scratch,
        compiler_params=_params(("arbitrary", "arbitrary"), block_bytes),
        name="proj_" + epilogue,
    )(*args)


def _flash_kernel(q_ref, k_ref, vt_ref, *rest, tk, n_cast):
    cast_in = rest[:n_cast]
    o_ref = rest[n_cast]
    cast_out = rest[n_cast + 1:2 * n_cast + 1]
    st_ref, pt_ref, acc_ref = rest[2 * n_cast + 1:]
    for src, dst in zip(cast_in, cast_out):
        dst[...] = src[...].astype(dst.dtype)
    tq = q_ref.shape[0]
    nchunks = k_ref.shape[0] // tk
    qs = [q_ref[:, h * HEAD_DIM:(h + 1) * HEAD_DIM] for h in range(Q_GROUP)]

    def k_chunk(c):
        return k_ref[pl.ds(pl.multiple_of(c * tk, tk), tk), :]

    def score(kc, h):
        return lax.dot_general(kc, qs[h], (((1,), (1,)), ((), ())), preferred_element_type=F32)

    def exact_softmax(c, ms, slot):
        kc = k_chunk(c)
        out = []
        for h in range(Q_GROUP):
            st = score(kc, h)
            st_ref[h] = st
            m_new = jnp.maximum(ms[h], jnp.max(st, axis=0, keepdims=True))
            alpha = jnp.exp2(ms[h] - m_new)
            for r in range(tk // SOFTMAX_ROWS):
                rows = slice(r * SOFTMAX_ROWS, (r + 1) * SOFTMAX_ROWS)
                pt_ref[slot, h, rows, :] = jnp.exp2(st_ref[h, rows, :] - m_new).astype(BF16)
            out.append((m_new, alpha))
        return tuple(zip(*out))

    def lagged_softmax(c, refs, slot):
        kc = k_chunk(c)
        cms = []
        for h in range(Q_GROUP):
            st = score(kc, h)
            pt_ref[slot, h] = jnp.exp2(st - refs[h]).astype(BF16)
            cms.append(jnp.max(st, axis=0, keepdims=True))
        return cms

    def advance(refs, excess, cms):
        new_refs = tuple(jnp.maximum(refs[h], cms[h]) for h in range(Q_GROUP))
        alphas = tuple(jnp.exp2(refs[h] - new_refs[h]) for h in range(Q_GROUP))
        excess = tuple(jnp.maximum(excess[h], cms[h] - refs[h]) for h in range(Q_GROUP))
        return new_refs, excess, alphas

    ones_rows = (lax.broadcasted_iota(jnp.int32, (BF16_SUBLANES, tk), 0) == 0).astype(BF16)

    def value_update(c, alphas, slot):
        vc = vt_ref[:, pl.ds(pl.multiple_of(c * tk, tk), tk)]
        vc = jnp.concatenate([vc, ones_rows], axis=0)
        for h in range(Q_GROUP):
            acc_ref[h] = alphas[h] * acc_ref[h] + jnp.dot(vc, pt_ref[slot, h],
                                                         preferred_element_type=F32)

    def finalize():
        for h in range(Q_GROUP):
            out_t = acc_ref[h, 0:HEAD_DIM, :] / acc_ref[h, HEAD_DIM:HEAD_DIM + 1, :]
            o_ref[:, h * HEAD_DIM:(h + 1) * HEAD_DIM] = out_t.T.astype(o_ref.dtype)

    def stage(c, k, carry):
        refs, excess, a_c2, a_c1, a_c = carry
        cms = lagged_softmax(c, refs, k % P_SLOTS)
        value_update(c - 2, a_c2, (k - 2) % P_SLOTS)
        refs, excess, a_next = advance(refs, excess, cms)
        return refs, excess, a_c1, a_c, a_next

    neg_inf = tuple(jnp.full((1, tq), -jnp.inf, F32) for _ in range(Q_GROUP))
    acc_ref[...] = jnp.zeros(acc_ref.shape, F32)
    refs, a0 = exact_softmax(0, neg_inf, 0)
    a1 = tuple(jnp.ones((1, tq), F32) for _ in range(Q_GROUP))
    cms = lagged_softmax(1, refs, 1)
    refs, excess, a2 = advance(refs, neg_inf, cms)
    carry = (refs, excess, a0, a1, a2)

    ntrips = (nchunks - 2) // P_SLOTS

    def body(t, carry):
        for k in range(2, 2 + P_SLOTS):
            carry = stage(P_SLOTS * t + k, k, carry)
        return carry

    carry = lax.fori_loop(0, ntrips, body, carry)
    for c in range(2 + P_SLOTS * ntrips, nchunks):
        carry = stage(c, c, carry)
    _, excess, a_c2, a_c1, _ = carry
    value_update(nchunks - 2, a_c2, (nchunks - 2) % P_SLOTS)
    value_update(nchunks - 1, a_c1, (nchunks - 1) % P_SLOTS)
    finalize()

    worst = excess[0]
    for h in range(1, Q_GROUP):
        worst = jnp.maximum(worst, excess[h])

    @pl.when(jnp.max(worst) > MAX_LAGGED_EXPONENT)
    def _():
        acc_ref[...] = jnp.zeros(acc_ref.shape, F32)

        def exact_body(c, ms):
            ms, alphas = exact_softmax(c, ms, 0)
            value_update(c, alphas, 0)
            return ms

        lax.fori_loop(0, nchunks, exact_body, neg_inf)
        finalize()


def _cast_block(shape, nsteps):
    rows, cols = shape
    ncol = 1
    while ncol <= nsteps:
        nrow = nsteps // ncol
        if (rows % (nrow * BF16_SUBLANES) == 0 and cols % (ncol * V7X_LANES) == 0
                and nrow * ncol == nsteps):
            return (rows // nrow, cols // ncol), ncol
        ncol *= 2
    raise ValueError(f"cannot tile {shape} over {nsteps} steps")


def _attention(q, k, vt, cast=(), tq=512, tk=512):
    s = q.shape[0]
    gw = Q_GROUP * HEAD_DIM
    nq = s // tq
    nsteps = N_KV_HEADS * nq
    block_bytes = (2 * (tq * gw * 2 + 2 * s * HEAD_DIM * 2 + tq * gw * 2)
                   + Q_GROUP * (tk * tq * (4 + 2 * P_SLOTS) + HEAD_DIM * tq * 4))
    cast_in_specs, cast_out_specs = [], []
    for w, layer in cast:
        blk, ncol = _cast_block(w.shape[1:], nsteps)
        cast_in_specs.append(pl.BlockSpec(
            (None,) + blk,
            lambda g, i, ncol=ncol, layer=layer: (layer, (g * nq + i) // ncol, (g * nq + i) % ncol)))
        cast_out_specs.append(pl.BlockSpec(
            blk, lambda g, i, ncol=ncol: ((g * nq + i) // ncol, (g * nq + i) % ncol)))
        block_bytes += 2 * blk[0] * blk[1] * (4 + 2)
    out = pl.pallas_call(
        functools.partial(_flash_kernel, tk=tk, n_cast=len(cast)),
        grid=(N_KV_HEADS, nq),
        in_specs=[pl.BlockSpec((tq, gw), lambda g, i: (i, g)),
                  pl.BlockSpec((s, HEAD_DIM), lambda g, i: (0, g)),
                  pl.BlockSpec((HEAD_DIM, s), lambda g, i: (g, 0))] + cast_in_specs,
        out_specs=[pl.BlockSpec((tq, gw), lambda g, i: (i, g))] + cast_out_specs,
        out_shape=[jax.ShapeDtypeStruct((s, ATTN_WIDTH), BF16)]
                  + [jax.ShapeDtypeStruct(w.shape[1:], BF16) for w, _ in cast],
        scratch_shapes=[pltpu.VMEM((Q_GROUP, tk, tq), F32),
                        pltpu.VMEM((P_SLOTS, Q_GROUP, tk, tq), BF16),
                        pltpu.VMEM((Q_GROUP, HEAD_DIM + BF16_SUBLANES, tq), F32)],
        compiler_params=_params(("arbitrary", "arbitrary"), block_bytes),
        name="flash_attention",
    )(q, k, vt, *[w for w, _ in cast])
    return out[0], out[1:]


def _pool_merge_kernel(up_ref, uc_ref, un_ref, wp_ref, ps_ref, gates_ref, attn_ref, o_ref, ext_ref,
                       *, seq):
    i = pl.program_id(0)
    tm = uc_ref.shape[0]
    last = pl.num_programs(0) - 1
    ext_ref[0:POOL_HALO, :] = jnp.where(i > 0, up_ref[...], 0.0)
    ext_ref[POOL_HALO:POOL_HALO + tm, :] = uc_ref[...]
    ext_ref[POOL_HALO + tm:, :] = jnp.where(i < last, un_ref[...], 0.0)
    t = i * tm + lax.broadcasted_iota(jnp.int32, (tm, 1), 0)
    for g in range(N_POOL_GROUPS):
        half = POOL_WINDOWS[g] // 2
        cols = slice(g * POOL_GROUP_IN, (g + 1) * POOL_GROUP_IN)
        wsum = ext_ref[POOL_HALO - half:POOL_HALO - half + tm, cols]
        for d in range(-half + 1, half):
            wsum = wsum + ext_ref[POOL_HALO + d:POOL_HALO + d + tm, cols]
        count = (jnp.minimum(t + half, seq) - jnp.maximum(t - half, 0)).astype(F32)
        delta = wsum / count - uc_ref[:, cols]
        y = jnp.dot(delta.astype(BF16), wp_ref[g], preferred_element_type=F32)
        oc = slice(g * POOL_GROUP_OUT, (g + 1) * POOL_GROUP_OUT)
        pool = y * ps_ref[:, oc]
        ga = gates_ref[:, oc].astype(F32)
        gp = gates_ref[:, D_MODEL + g * POOL_GROUP_OUT:D_MODEL + (g + 1) * POOL_GROUP_OUT].astype(F32)
        o_ref[:, oc] = (ga * attn_ref[:, oc].astype(F32) + gp * pool).astype(o_ref.dtype)


def _pool_merge(u, w_pool, pool_scale, gates, attn, tm=256):
    s = u.shape[0]
    hb = tm // POOL_HALO
    nhb = s // POOL_HALO
    block_bytes = (2 * (tm * POOL_WIDTH * 4 + 2 * POOL_HALO * POOL_WIDTH * 4
                        + tm * 2 * D_MODEL * gates.dtype.itemsize
                        + tm * D_MODEL * attn.dtype.itemsize + tm * D_MODEL * 2 + w_pool.size * 2)
                   + (tm + 2 * POOL_HALO) * POOL_WIDTH * 4)
    return pl.pallas_call(
        functools.partial(_pool_merge_kernel, seq=s),
        grid=(s // tm,),
        in_specs=[pl.BlockSpec((POOL_HALO, POOL_WIDTH), lambda i: (jnp.maximum(i * hb - 1, 0), 0)),
                  pl.BlockSpec((tm, POOL_WIDTH), lambda i: (i, 0)),
                  pl.BlockSpec((POOL_HALO, POOL_WIDTH), lambda i: (jnp.minimum((i + 1) * hb, nhb - 1), 0)),
                  pl.BlockSpec(w_pool.shape, lambda i: (0, 0, 0)),
                  pl.BlockSpec((1, D_MODEL), lambda i: (0, 0)),
                  pl.BlockSpec((tm, 2 * D_MODEL), lambda i: (i, 0)),
                  pl.BlockSpec((tm, D_MODEL), lambda i: (i, 0))],
        out_specs=pl.BlockSpec((tm, D_MODEL), lambda i: (i, 0)),
        out_shape=jax.ShapeDtypeStruct((s, D_MODEL), BF16),
        scratch_shapes=[pltpu.VMEM((tm + 2 * POOL_HALO, POOL_WIDTH), F32)],
        compiler_params=_params(("arbitrary",), block_bytes),
        name="pool_merge",
    )(u, u, u, w_pool, pool_scale.reshape(1, D_MODEL), gates, attn)


def _ffn_kernel(h_ref, wg_ref, wu_ref, wd_ref, o_ref):
    @pl.when(pl.program_id(1) == 0)
    def _():
        o_ref[...] = jnp.zeros(o_ref.shape, F32)

    h = h_ref[...]
    g = jnp.dot(h, wg_ref[...], preferred_element_type=F32)
    u = jnp.dot(h, wu_ref[...], preferred_element_type=F32)
    a = (g * jax.nn.sigmoid(g) * u).astype(BF16)
    o_ref[...] += jnp.dot(a, wd_ref[...], preferred_element_type=F32)


def _ffn(h, wg, wu, wd, tm=512, tf=256):
    s, d = h.shape
    dff = wg.shape[1]
    assert dff % tf == 0
    block_bytes = 2 * (tm * d * 2 + 3 * d * tf * 2 + tm * d * 4)
    return pl.pallas_call(
        _ffn_kernel,
        grid=(s // tm, dff // tf),
        in_specs=[pl.BlockSpec((tm, d), lambda i, f: (i, 0)),
                  pl.BlockSpec((d, tf), lambda i, f: (0, f)),
                  pl.BlockSpec((d, tf), lambda i, f: (0, f)),
                  pl.BlockSpec((tf, d), lambda i, f: (f, 0))],
        out_specs=pl.BlockSpec((tm, d), lambda i, f: (i, 0)),
        out_shape=jax.ShapeDtypeStruct((s, d), F32),
        compiler_params=_params(("arbitrary", "arbitrary"), block_bytes),
        name="swiglu_ffn",
    )(h, wg, wu, wd)


def _rope_tables(seq):
    rows = seq // GRID_W
    row = jnp.broadcast_to(jnp.arange(rows)[:, None], (rows, GRID_W)).reshape(seq)
    col = jnp.broadcast_to(jnp.arange(GRID_W)[None, :], (rows, GRID_W)).reshape(seq)
    inv_freq = ROPE_THETA ** (-jnp.arange(0, AXIS_ROT_DIM, 2, dtype=F32) / AXIS_ROT_DIM)
    pos = jnp.stack([row, col], axis=-1).astype(F32)
    ang = pos[:, :, None] * inv_freq[None, None, :]
    cos, sin = jnp.cos(ang), jnp.sin(ang)
    cos_t = jnp.concatenate([cos, cos], axis=-1).reshape(seq, HEAD_DIM)
    sin_t = jnp.concatenate([-sin, sin], axis=-1).reshape(seq, HEAD_DIM)
    return cos_t, sin_t


def kernel(x, p, w_in, q_norm, k_norm, w_pool, pool_scale, w_out, norm_mix_pre, norm_mix_post,
           norm_ffn_pre, norm_ffn_post, w_ffn_gate, w_ffn_up, w_ffn_down, w_ple_in, w_ple_gate,
           norm_ple):
    b, s, d = x.shape
    assert b == 1 and d == D_MODEL and s % GRID_W == 0
    cos_t, sin_t = _rope_tables(s)
    xs = x.reshape(s, d)
    h = _norm_cast(xs, norm_mix_pre[0])
    w_in_b = w_in[0].astype(BF16)
    for i in range(DEPTH):
        q = _proj(h, w_in_b, COL_Q, ATTN_WIDTH, BF16, "qk",
                  (q_norm[i] * (HEAD_DIM ** -0.5 * LOG2_E), cos_t, sin_t))
        k = _proj(h, w_in_b, COL_K, KV_WIDTH, BF16, "qk", (k_norm[i], cos_t, sin_t))
        vt = _proj(h, w_in_b, COL_V, KV_WIDTH, BF16, "transpose")
        u = _proj(h, w_in_b, COL_U, POOL_WIDTH, F32)
        gates = _proj(h, w_in_b, COL_GATES, 2 * D_MODEL, BF16, "sigmoid")
        later = [(w_out, i), (w_ffn_gate, i), (w_ffn_up, i), (w_ffn_down, i), (w_ple_gate, i)]
        if i + 1 < DEPTH:
            later.append((w_in, i + 1))
        attn, later_b = _attention(q, k, vt, later)
        w_out_b, w_gate_b, w_up_b, w_down_b, w_ple_gate_b = later_b[:5]
        if i + 1 < DEPTH:
            w_in_b = later_b[5]
        merged = _pool_merge(u, w_pool[i].astype(BF16), pool_scale[i], gates, attn)
        o = _proj(merged, w_out_b, 0, D_MODEL, F32)
        xs, h = _resnorm(xs, o, norm_mix_post[i], norm_ffn_pre[i], "norm")
        f = _ffn(h, w_gate_b, w_up_b, w_down_b)
        xs, xb = _resnorm(xs, f, norm_ffn_post[i], None, "cast")
        ge = _proj(xb, w_ple_gate_b, 0, D_MODEL, F32, "ple",
                   (p[i].reshape(s, D_PLE).astype(BF16), w_ple_in[i].astype(BF16)))
        if i + 1 < DEPTH:
            xs, h = _resnorm(xs, ge, norm_ple[i], norm_mix_pre[i + 1], "norm")
        else:
            xs, _ = _resnorm(xs, ge, norm_ple[i], None, "none")
    return xs.reshape(b, s, d)
```

```python
import functools

import jax
import jax.numpy as jnp
from jax import lax
from jax.experimental import pallas as pl
from jax.experimental.pallas import tpu as pltpu

D_MODEL = 4096
SEQ = 8192
DEPTH = 2
N_HEADS = 32
N_KV_HEADS = 8
HEAD_DIM = 128
Q_GROUP = N_HEADS // N_KV_HEADS
ATTN_WIDTH = N_HEADS * HEAD_DIM
KV_WIDTH = N_KV_HEADS * HEAD_DIM
AXIS_ROT_DIM = HEAD_DIM // 2
ROT_HALF = AXIS_ROT_DIM // 2
ROPE_THETA = 10000.0
GRID_W = 64
POOL_WINDOWS = (2, 4, 8, 16)
N_POOL_GROUPS = 4
POOL_WIDTH = 2048
POOL_GROUP_IN = POOL_WIDTH // N_POOL_GROUPS
POOL_GROUP_OUT = D_MODEL // N_POOL_GROUPS
D_FF = 11008
D_PLE = 256
EPS = 1e-6
LOG2_E = 1.4426950408889634

COL_Q = 0
COL_K = COL_Q + ATTN_WIDTH
COL_V = COL_K + KV_WIDTH
COL_U = COL_V + KV_WIDTH
COL_GATES = COL_U + POOL_WIDTH

V7X_VMEM_BYTES = 64 * 1024 * 1024
V7X_SUBLANES = 8
V7X_LANES = 128
BF16_SUBLANES = 2 * V7X_SUBLANES
POOL_HALO = 8
SOFTMAX_ROWS = 32
QK_EPILOGUE_ROWS = 128
PROJ_MATMUL_ROWS = 256
MAX_LAGGED_EXPONENT = 64.0
P_SLOTS = 3
VMEM_TEMP_BYTES = 12 * 1024 * 1024

F32 = jnp.float32
BF16 = jnp.bfloat16


def _params(semantics, block_bytes):
    limit = min(int(block_bytes) + VMEM_TEMP_BYTES, V7X_VMEM_BYTES - 4 * 1024 * 1024)
    return pltpu.CompilerParams(dimension_semantics=semantics, vmem_limit_bytes=limit)


def _rms(x, gain):
    ms = jnp.mean(x * x, axis=-1, keepdims=True)
    return x * lax.rsqrt(ms + EPS) * gain


def _norm_cast_kernel(x_ref, g_ref, o_ref):
    o_ref[...] = _rms(x_ref[...], g_ref[...]).astype(o_ref.dtype)


def _norm_cast(x, gain, tm=256):
    s, d = x.shape
    return pl.pallas_call(
        _norm_cast_kernel,
        grid=(s // tm,),
        in_specs=[pl.BlockSpec((tm, d), lambda i: (i, 0)),
                  pl.BlockSpec((1, d), lambda i: (0, 0))],
        out_specs=pl.BlockSpec((tm, d), lambda i: (i, 0)),
        out_shape=jax.ShapeDtypeStruct((s, d), BF16),
        compiler_params=_params(("arbitrary",), 2 * tm * d * 6),
        name="norm_cast",
    )(x, gain.reshape(1, d))


def _resnorm_kernel(x_ref, f_ref, gpost_ref, *rest, mode):
    xn = x_ref[...] + _rms(f_ref[...], gpost_ref[...])
    if mode == "norm":
        gnext_ref, xo_ref, ho_ref = rest
        ho_ref[...] = _rms(xn, gnext_ref[...]).astype(ho_ref.dtype)
    elif mode == "cast":
        xo_ref, ho_ref = rest
        ho_ref[...] = xn.astype(ho_ref.dtype)
    else:
        (xo_ref,) = rest
    xo_ref[...] = xn


def _resnorm(x, f, gpost, gnext=None, mode="none", tm=256):
    s, d = x.shape
    row = pl.BlockSpec((tm, d), lambda i: (i, 0))
    vec = pl.BlockSpec((1, d), lambda i: (0, 0))
    in_specs = [row, row, vec]
    args = [x, f, gpost.reshape(1, d)]
    if mode == "norm":
        in_specs.append(vec)
        args.append(gnext.reshape(1, d))
    out_shape = [jax.ShapeDtypeStruct((s, d), F32)]
    out_specs = [row]
    if mode != "none":
        out_shape.append(jax.ShapeDtypeStruct((s, d), BF16))
        out_specs.append(row)
    out = pl.pallas_call(
        functools.partial(_resnorm_kernel, mode=mode),
        grid=(s // tm,),
        in_specs=in_specs,
        out_specs=out_specs,
        out_shape=out_shape,
        compiler_params=_params(("arbitrary",), 2 * tm * d * 14),
        name="resnorm_" + mode,
    )(*args)
    return out if mode != "none" else (out[0], None)


def _proj_kernel(a_ref, w_ref, *rest, epilogue):
    o_ref = rest[-1] if epilogue != "qk" else rest[-2]
    w = w_ref[...]
    if epilogue == "qk":
        g_ref, cos_ref, sin_ref, _, acc_ref = rest
        lane = lax.broadcasted_iota(jnp.int32, (1, HEAD_DIM), 1)
        first_half = (lane & ROT_HALF) == 0
        gain = g_ref[...]
    for mb in range(a_ref.shape[0] // PROJ_MATMUL_ROWS):
        mrows = slice(mb * PROJ_MATMUL_ROWS, (mb + 1) * PROJ_MATMUL_ROWS)
        acc = jnp.dot(a_ref[mrows, :], w, preferred_element_type=F32)
        if epilogue == "none":
            o_ref[mrows, :] = acc.astype(o_ref.dtype)
        elif epilogue == "transpose":
            o_ref[:, mrows] = acc.T.astype(o_ref.dtype)
        elif epilogue == "sigmoid":
            o_ref[mrows, :] = jax.nn.sigmoid(acc).astype(o_ref.dtype)
        elif epilogue == "ple":
            p_ref, wp_ref, _ = rest
            e = jnp.dot(p_ref[mrows, :], wp_ref[...], preferred_element_type=F32)
            o_ref[mrows, :] = (jax.nn.sigmoid(acc) * e).astype(o_ref.dtype)
        elif epilogue == "qk":
            acc_ref[mrows, :] = acc
            for r in range(PROJ_MATMUL_ROWS // QK_EPILOGUE_ROWS):
                start = mb * PROJ_MATMUL_ROWS + r * QK_EPILOGUE_ROWS
                rows = slice(start, start + QK_EPILOGUE_ROWS)
                cos = cos_ref[rows, :]
                sin = sin_ref[rows, :]
                for h in range(acc_ref.shape[1] // HEAD_DIM):
                    sl = slice(h * HEAD_DIM, (h + 1) * HEAD_DIM)
                    y = _rms(acc_ref[rows, sl], gain)
                    partner = jnp.where(first_half,
                                        pltpu.roll(y, HEAD_DIM - ROT_HALF, 1),
                                        pltpu.roll(y, ROT_HALF, 1))
                    o_ref[rows, sl] = (y * cos + partner * sin).astype(o_ref.dtype)
        else:
            raise ValueError(epilogue)


def _proj(a, w, col0, ncols, out_dtype, epilogue="none", extra=(), tm=1024, tn=1024):
    m, k = a.shape
    tn = min(tn, ncols)
    assert m % tm == 0 and ncols % tn == 0 and col0 % tn == 0 and w.shape[0] == k
    cb = col0 // tn
    in_specs = [pl.BlockSpec((tm, k), lambda i, j: (i, 0)),
                pl.BlockSpec((k, tn), lambda i, j: (0, cb + j))]
    args = [a, w]
    scratch = []
    extra_bytes = 0
    if epilogue == "qk":
        gain, cos, sin = extra
        in_specs += [pl.BlockSpec((1, HEAD_DIM), lambda i, j: (0, 0)),
                     pl.BlockSpec((tm, HEAD_DIM), lambda i, j: (i, 0)),
                     pl.BlockSpec((tm, HEAD_DIM), lambda i, j: (i, 0))]
        args += [gain.reshape(1, HEAD_DIM), cos, sin]
        scratch = [pltpu.VMEM((tm, tn), F32)]
        extra_bytes = tm * tn * 4 + 4 * tm * HEAD_DIM * 4
    elif epilogue == "ple":
        p, wp = extra
        kp = p.shape[1]
        in_specs += [pl.BlockSpec((tm, kp), lambda i, j: (i, 0)),
                     pl.BlockSpec((kp, tn), lambda i, j: (0, j))]
        args += [p, wp]
        extra_bytes = 2 * (tm * kp + kp * tn) * 2
    out_bytes = jnp.dtype(out_dtype).itemsize
    block_bytes = 2 * (tm * k * 2 + k * tn * 2 + tm * tn * out_bytes) + extra_bytes
    if epilogue == "transpose":
        out_spec = pl.BlockSpec((tn, tm), lambda i, j: (j, i))
        out_shape = jax.ShapeDtypeStruct((ncols, m), out_dtype)
    else:
        out_spec = pl.BlockSpec((tm, tn), lambda i, j: (i, j))
        out_shape = jax.ShapeDtypeStruct((m, ncols), out_dtype)
    return pl.pallas_call(
        functools.partial(_proj_kernel, epilogue=epilogue),
        grid=(m // tm, ncols // tn),
        in_specs=in_specs,
        out_specs=out_spec,
        out_shape=out_shape,
        scratch_shapes=scratch,
        compiler_params=_params(("arbitrary", "arbitrary"), block_bytes),
        name="proj_" + epilogue,
    )(*args)


def _flash_kernel(q_ref, k_ref, vt_ref, *rest, tk, n_cast):
    cast_in = rest[:n_cast]
    o_ref = rest[n_cast]
    cast_out = rest[n_cast + 1:2 * n_cast + 1]
    st_ref, pt_ref, acc_ref = rest[2 * n_cast + 1:]
    for src, dst in zip(cast_in, cast_out):
        dst[...] = src[...].astype(dst.dtype)
    tq = q_ref.shape[0]
    nchunks = k_ref.shape[0] // tk
    qs = [q_ref[:, h * HEAD_DIM:(h + 1) * HEAD_DIM] for h in range(Q_GROUP)]

    def k_chunk(c):
        return k_ref[pl.ds(pl.multiple_of(c * tk, tk), tk), :]

    def score(kc, h):
        return lax.dot_general(kc, qs[h], (((1,), (1,)), ((), ())), preferred_element_type=F32)

    def exact_softmax(c, ms, slot):
        kc = k_chunk(c)
        out = []
        for h in range(Q_GROUP):
            st = score(kc, h)
            st_ref[h] = st
            m_new = jnp.maximum(ms[h], jnp.max(st, axis=0, keepdims=True))
            alpha = jnp.exp2(ms[h] - m_new)
            for r in range(tk // SOFTMAX_ROWS):
                rows = slice(r * SOFTMAX_ROWS, (r + 1) * SOFTMAX_ROWS)
                pt_ref[slot, h, rows, :] = jnp.exp2(st_ref[h, rows, :] - m_new).astype(BF16)
            out.append((m_new, alpha))
        return tuple(zip(*out))

    def lagged_softmax(c, refs, slot):
        kc = k_chunk(c)
        cms = []
        for h in range(Q_GROUP):
            st = score(kc, h)
            pt_ref[slot, h] = jnp.exp2(st - refs[h]).astype(BF16)
            cms.append(jnp.max(st, axis=0, keepdims=True))
        return cms

    def advance(refs, excess, cms):
        new_refs = tuple(jnp.maximum(refs[h], cms[h]) for h in range(Q_GROUP))
        alphas = tuple(jnp.exp2(refs[h] - new_refs[h]) for h in range(Q_GROUP))
        excess = tuple(jnp.maximum(excess[h], cms[h] - refs[h]) for h in range(Q_GROUP))
        return new_refs, excess, alphas

    ones_rows = (lax.broadcasted_iota(jnp.int32, (BF16_SUBLANES, tk), 0) == 0).astype(BF16)

    def value_update(c, alphas, slot):
        vc = vt_ref[:, pl.ds(pl.multiple_of(c * tk, tk), tk)]
        vc = jnp.concatenate([vc, ones_rows], axis=0)
        for h in range(Q_GROUP):
            acc_ref[h] = alphas[h] * acc_ref[h] + jnp.dot(vc, pt_ref[slot, h],
                                                         preferred_element_type=F32)

    def finalize():
        for h in range(Q_GROUP):
            out_t = acc_ref[h, 0:HEAD_DIM, :] / acc_ref[h, HEAD_DIM:HEAD_DIM + 1, :]
            o_ref[:, h * HEAD_DIM:(h + 1) * HEAD_DIM] = out_t.T.astype(o_ref.dtype)

    def stage(c, k, carry):
        refs, excess, a_c2, a_c1, a_c = carry
        cms = lagged_softmax(c, refs, k % P_SLOTS)
        value_update(c - 2, a_c2, (k - 2) % P_SLOTS)
        refs, excess, a_next = advance(refs, excess, cms)
        return refs, excess, a_c1, a_c, a_next

    neg_inf = tuple(jnp.full((1, tq), -jnp.inf, F32) for _ in range(Q_GROUP))
    acc_ref[...] = jnp.zeros(acc_ref.shape, F32)
    refs, a0 = exact_softmax(0, neg_inf, 0)
    a1 = tuple(jnp.ones((1, tq), F32) for _ in range(Q_GROUP))
    cms = lagged_softmax(1, refs, 1)
    refs, excess, a2 = advance(refs, neg_inf, cms)
    carry = (refs, excess, a0, a1, a2)

    ntrips = (nchunks - 2) // P_SLOTS

    def body(t, carry):
        for k in range(2, 2 + P_SLOTS):
            carry = stage(P_SLOTS * t + k, k, carry)
        return carry

    carry = lax.fori_loop(0, ntrips, body, carry)
    for c in range(2 + P_SLOTS * ntrips, nchunks):
        carry = stage(c, c, carry)
    _, excess, a_c2, a_c1, _ = carry
    value_update(nchunks - 2, a_c2, (nchunks - 2) % P_SLOTS)
    value_update(nchunks - 1, a_c1, (nchunks - 1) % P_SLOTS)
    finalize()

    worst = excess[0]
    for h in range(1, Q_GROUP):
        worst = jnp.maximum(worst, excess[h])

    @pl.when(jnp.max(worst) > MAX_LAGGED_EXPONENT)
    def _():
        acc_ref[...] = jnp.zeros(acc_ref.shape, F32)

        def exact_body(c, ms):
            ms, alphas = exact_softmax(c, ms, 0)
            value_update(c, alphas, 0)
            return ms

        lax.fori_loop(0, nchunks, exact_body, neg_inf)
        finalize()


def _cast_block(shape, nsteps):
    rows, cols = shape
    ncol = 1
    while ncol <= nsteps:
        nrow = nsteps // ncol
        if (rows % (nrow * BF16_SUBLANES) == 0 and cols % (ncol * V7X_LANES) == 0
                and nrow * ncol == nsteps):
            return (rows // nrow, cols // ncol), ncol
        ncol *= 2
    raise ValueError(f"cannot tile {shape} over {nsteps} steps")


def _attention(q, k, vt, cast=(), tq=512, tk=512):
    s = q.shape[0]
    gw = Q_GROUP * HEAD_DIM
    nq = s // tq
    nsteps = N_KV_HEADS * nq
    block_bytes = (2 * (tq * gw * 2 + 2 * s * HEAD_DIM * 2 + tq * gw * 2)
                   + Q_GROUP * (tk * tq * (4 + 2 * P_SLOTS) + HEAD_DIM * tq * 4))
    cast_in_specs, cast_out_specs = [], []
    for w, layer in cast:
        blk, ncol = _cast_block(w.shape[1:], nsteps)
        cast_in_specs.append(pl.BlockSpec(
            (None,) + blk,
            lambda g, i, ncol=ncol, layer=layer: (layer, (g * nq + i) // ncol, (g * nq + i) % ncol)))
        cast_out_specs.append(pl.BlockSpec(
            blk, lambda g, i, ncol=ncol: ((g * nq + i) // ncol, (g * nq + i) % ncol)))
        block_bytes += 2 * blk[0] * blk[1] * (4 + 2)
    out = pl.pallas_call(
        functools.partial(_flash_kernel, tk=tk, n_cast=len(cast)),
        grid=(N_KV_HEADS, nq),
        in_specs=[pl.BlockSpec((tq, gw), lambda g, i: (i, g)),
                  pl.BlockSpec((s, HEAD_DIM), lambda g, i: (0, g)),
                  pl.BlockSpec((HEAD_DIM, s), lambda g, i: (g, 0))] + cast_in_specs,
        out_specs=[pl.BlockSpec((tq, gw), lambda g, i: (i, g))] + cast_out_specs,
        out_shape=[jax.ShapeDtypeStruct((s, ATTN_WIDTH), BF16)]
                  + [jax.ShapeDtypeStruct(w.shape[1:], BF16) for w, _ in cast],
        scratch_shapes=[pltpu.VMEM((Q_GROUP, tk, tq), F32),
                        pltpu.VMEM((P_SLOTS, Q_GROUP, tk, tq), BF16),
                        pltpu.VMEM((Q_GROUP, HEAD_DIM + BF16_SUBLANES, tq), F32)],
        compiler_params=_params(("arbitrary", "arbitrary"), block_bytes),
        name="flash_attention",
    )(q, k, vt, *[w for w, _ in cast])
    return out[0], out[1:]


def _pool_merge_kernel(up_ref, uc_ref, un_ref, wp_ref, ps_ref, gates_ref, attn_ref, o_ref, ext_ref,
                       *, seq):
    i = pl.program_id(0)
    tm = uc_ref.shape[0]
    last = pl.num_programs(0) - 1
    ext_ref[0:POOL_HALO, :] = jnp.where(i > 0, up_ref[...], 0.0)
    ext_ref[POOL_HALO:POOL_HALO + tm, :] = uc_ref[...]
    ext_ref[POOL_HALO + tm:, :] = jnp.where(i < last, un_ref[...], 0.0)
    t = i * tm + lax.broadcasted_iota(jnp.int32, (tm, 1), 0)
    for g in range(N_POOL_GROUPS):
        half = POOL_WINDOWS[g] // 2
        cols = slice(g * POOL_GROUP_IN, (g + 1) * POOL_GROUP_IN)
        wsum = ext_ref[POOL_HALO - half:POOL_HALO - half + tm, cols]
        for d in range(-half + 1, half):
            wsum = wsum + ext_ref[POOL_HALO + d:POOL_HALO + d + tm, cols]
        count = (jnp.minimum(t + half, seq) - jnp.maximum(t - half, 0)).astype(F32)
        delta = wsum / count - uc_ref[:, cols]
        y = jnp.dot(delta.astype(BF16), wp_ref[g], preferred_element_type=F32)
        oc = slice(g * POOL_GROUP_OUT, (g + 1) * POOL_GROUP_OUT)
        pool = y * ps_ref[:, oc]
        ga = gates_ref[:, oc].astype(F32)
        gp = gates_ref[:, D_MODEL + g * POOL_GROUP_OUT:D_MODEL + (g + 1) * POOL_GROUP_OUT].astype(F32)
        o_ref[:, oc] = (ga * attn_ref[:, oc].astype(F32) + gp * pool).astype(o_ref.dtype)


def _pool_merge(u, w_pool, pool_scale, gates, attn, tm=256):
    s = u.shape[0]
    hb = tm // POOL_HALO
    nhb = s // POOL_HALO
    block_bytes = (2 * (tm * POOL_WIDTH * 4 + 2 * POOL_HALO * POOL_WIDTH * 4
                        + tm * 2 * D_MODEL * gates.dtype.itemsize
                        + tm * D_MODEL * attn.dtype.itemsize + tm * D_MODEL * 2 + w_pool.size * 2)
                   + (tm + 2 * POOL_HALO) * POOL_WIDTH * 4)
    return pl.pallas_call(
        functools.partial(_pool_merge_kernel, seq=s),
        grid=(s // tm,),
        in_specs=[pl.BlockSpec((POOL_HALO, POOL_WIDTH), lambda i: (jnp.maximum(i * hb - 1, 0), 0)),
                  pl.BlockSpec((tm, POOL_WIDTH), lambda i: (i, 0)),
                  pl.BlockSpec((POOL_HALO, POOL_WIDTH), lambda i: (jnp.minimum((i + 1) * hb, nhb - 1), 0)),
                  pl.BlockSpec(w_pool.shape, lambda i: (0, 0, 0)),
                  pl.BlockSpec((1, D_MODEL), lambda i: (0, 0)),
                  pl.BlockSpec((tm, 2 * D_MODEL), lambda i: (i, 0)),
                  pl.BlockSpec((tm, D_MODEL), lambda i: (i, 0))],
        out_specs=pl.BlockSpec((tm, D_MODEL), lambda i: (i, 0)),
        out_shape=jax.ShapeDtypeStruct((s, D_MODEL), BF16),
        scratch_shapes=[pltpu.VMEM((tm + 2 * POOL_HALO, POOL_WIDTH), F32)],
        compiler_params=_params(("arbitrary",), block_bytes),
        name="pool_merge",
    )(u, u, u, w_pool, pool_scale.reshape(1, D_MODEL), gates, attn)


def _ffn_kernel(h_ref, wg_ref, wu_ref, wd_ref, o_ref):
    @pl.when(pl.program_id(1) == 0)
    def _():
        o_ref[...] = jnp.zeros(o_ref.shape, F32)

    h = h_ref[...]
    g = jnp.dot(h, wg_ref[...], preferred_element_type=F32)
    u = jnp.dot(h, wu_ref[...], preferred_element_type=F32)
    a = (g * jax.nn.sigmoid(g) * u).astype(BF16)
    o_ref[...] += jnp.dot(a, wd_ref[...], preferred_element_type=F32)


def _ffn(h, wg, wu, wd, tm=512, tf=256):
    s, d = h.shape
    dff = wg.shape[1]
    assert dff % tf == 0
    block_bytes = 2 * (tm * d * 2 + 3 * d * tf * 2 + tm * d * 4)
    return pl.pallas_call(
        _ffn_kernel,
        grid=(s // tm, dff // tf),
        in_specs=[pl.BlockSpec((tm, d), lambda i, f: (i, 0)),
                  pl.BlockSpec((d, tf), lambda i, f: (0, f)),
                  pl.BlockSpec((d, tf), lambda i, f: (0, f)),
                  pl.BlockSpec((tf, d), lambda i, f: (f, 0))],
        out_specs=pl.BlockSpec((tm, d), lambda i, f: (i, 0)),
        out_shape=jax.ShapeDtypeStruct((s, d), F32),
        compiler_params=_params(("arbitrary", "arbitrary"), block_bytes),
        name="swiglu_ffn",
    )(h, wg, wu, wd)


def _rope_tables(seq):
    rows = seq // GRID_W
    row = jnp.broadcast_to(jnp.arange(rows)[:, None], (rows, GRID_W)).reshape(seq)
    col = jnp.broadcast_to(jnp.arange(GRID_W)[None, :], (rows, GRID_W)).reshape(seq)
    inv_freq = ROPE_THETA ** (-jnp.arange(0, AXIS_ROT_DIM, 2, dtype=F32) / AXIS_ROT_DIM)
    pos = jnp.stack([row, col], axis=-1).astype(F32)
    ang = pos[:, :, None] * inv_freq[None, None, :]
    cos, sin = jnp.cos(ang), jnp.sin(ang)
    cos_t = jnp.concatenate([cos, cos], axis=-1).reshape(seq, HEAD_DIM)
    sin_t = jnp.concatenate([-sin, sin], axis=-1).reshape(seq, HEAD_DIM)
    return cos_t, sin_t


def kernel(x, p, w_in, q_norm, k_norm, w_pool, pool_scale, w_out, norm_mix_pre, norm_mix_post,
           norm_ffn_pre, norm_ffn_post, w_ffn_gate, w_ffn_up, w_ffn_down, w_ple_in, w_ple_gate,
           norm_ple):
    b, s, d = x.shape
    assert b == 1 and d == D_MODEL and s % GRID_W == 0
    cos_t, sin_t = _rope_tables(s)
    xs = x.reshape(s, d)
    h = _norm_cast(xs, norm_mix_pre[0])
    w_in_b = w_in[0].astype(BF16)
    for i in range(DEPTH):
        q = _proj(h, w_in_b, COL_Q, ATTN_WIDTH, BF16, "qk",
                  (q_norm[i] * (HEAD_DIM ** -0.5 * LOG2_E), cos_t, sin_t))
        k = _proj(h, w_in_b, COL_K, KV_WIDTH, BF16, "qk", (k_norm[i], cos_t, sin_t))
        vt = _proj(h, w_in_b, COL_V, KV_WIDTH, BF16, "transpose")
        u = _proj(h, w_in_b, COL_U, POOL_WIDTH, F32)
        gates = _proj(h, w_in_b, COL_GATES, 2 * D_MODEL, BF16, "sigmoid")
        later = [(w_out, i), (w_ffn_gate, i), (w_ffn_up, i), (w_ffn_down, i), (w_ple_gate, i)]
        if i + 1 < DEPTH:
            later.append((w_in, i + 1))
        attn, later_b = _attention(q, k, vt, later)
        w_out_b, w_gate_b, w_up_b, w_down_b, w_ple_gate_b = later_b[:5]
        if i + 1 < DEPTH:
            w_in_b = later_b[5]
        merged = _pool_merge(u, w_pool[i].astype(BF16), pool_scale[i], gates, attn)
        o = _proj(merged, w_out_b, 0, D_MODEL, F32)
        xs, h = _resnorm(xs, o, norm_mix_post[i], norm_ffn_pre[i], "norm")
        f = _ffn(h, w_gate_b, w_up_b, w_down_b)
        xs, xb = _resnorm(xs, f, norm_ffn_post[i], None, "cast")
        ge = _proj(xb, w_ple_gate_b, 0, D_MODEL, F32, "ple",
                   (p[i].reshape(s, D_PLE).astype(BF16), w_ple_in[i].astype(BF16)))
        if i + 1 < DEPTH:
            xs, h = _resnorm(xs, ge, norm_ple[i], norm_mix_pre[i + 1], "norm")
        else:
            xs, _ = _resnorm(xs, ge, norm_ple[i], None, "none")
    return xs.reshape(b, s, d)
```

```python
import functools

import jax
import jax.numpy as jnp
from jax import lax
from jax.experimental import pallas as pl
from jax.experimental.pallas import tpu as pltpu

D_MODEL = 4096
DEPTH = 2
N_HEADS = 32
N_KV_HEADS = 8
HEAD_DIM = 128
Q_GROUP = N_HEADS // N_KV_HEADS
ATTN_WIDTH = N_HEADS * HEAD_DIM
KV_WIDTH = N_KV_HEADS * HEAD_DIM
AXIS_ROT_DIM = HEAD_DIM // 2
ROT_HALF = AXIS_ROT_DIM // 2
ROPE_THETA = 10000.0
GRID_W = 64
POOL_WINDOWS = (2, 4, 8, 16)
N_POOL_GROUPS = 4
POOL_WIDTH = 2048
POOL_GROUP_IN = POOL_WIDTH // N_POOL_GROUPS
POOL_GROUP_OUT = D_MODEL // N_POOL_GROUPS
D_PLE = 256
EPS = 1e-6
LOG2_E = 1.4426950408889634

COL_Q = 0
COL_K = COL_Q + ATTN_WIDTH
COL_V = COL_K + KV_WIDTH
COL_U = COL_V + KV_WIDTH
COL_GATES = COL_U + POOL_WIDTH

MIB = 1024 * 1024
V7X_VMEM_BYTES = 64 * MIB
V7X_VMEM_RESERVED_BYTES = 4 * MIB
V7X_SUBLANES = 8
V7X_LANES = 128
V7X_MXU_ROWS = 256
BF16_SUBLANES = 2 * V7X_SUBLANES
VMEM_TEMP_BYTES = 12 * MIB

ROW_TILE = 256
PROJ_TILE_M = 1024
PROJ_TILE_N = 1024
ATTN_TILE_Q = 512
ATTN_TILE_KV = 512
FFN_TILE_M = 512
FFN_TILE_FF = 256

POOL_HALO = max(POOL_WINDOWS) // 2
SOFTMAX_ROWS = 32
QK_EPILOGUE_ROWS = 128
PROJ_MATMUL_ROWS = V7X_MXU_ROWS
MAX_LAGGED_EXPONENT = 64.0
P_SLOTS = 3

F32 = jnp.float32
BF16 = jnp.bfloat16


def _params(semantics, block_bytes):
    limit = min(int(block_bytes) + VMEM_TEMP_BYTES, V7X_VMEM_BYTES - V7X_VMEM_RESERVED_BYTES)
    return pltpu.CompilerParams(dimension_semantics=semantics, vmem_limit_bytes=limit)


def _rms(x, gain):
    ms = jnp.mean(x * x, axis=-1, keepdims=True)
    return x * lax.rsqrt(ms + EPS) * gain


def _norm_cast_kernel(x_ref, g_ref, o_ref):
    o_ref[...] = _rms(x_ref[...], g_ref[...]).astype(o_ref.dtype)


def _norm_cast(x, gain, tm=ROW_TILE):
    s, d = x.shape
    return pl.pallas_call(
        _norm_cast_kernel,
        grid=(s // tm,),
        in_specs=[pl.BlockSpec((tm, d), lambda i: (i, 0)),
                  pl.BlockSpec((1, d), lambda i: (0, 0))],
        out_specs=pl.BlockSpec((tm, d), lambda i: (i, 0)),
        out_shape=jax.ShapeDtypeStruct((s, d), BF16),
        compiler_params=_params(("arbitrary",), 2 * tm * d * 6),
        name="norm_cast",
    )(x, gain.reshape(1, d))


def _resnorm_kernel(x_ref, f_ref, gpost_ref, *rest, mode):
    xn = x_ref[...] + _rms(f_ref[...], gpost_ref[...])
    if mode == "norm":
        gnext_ref, xo_ref, ho_ref = rest
        ho_ref[...] = _rms(xn, gnext_ref[...]).astype(ho_ref.dtype)
    elif mode == "cast":
        xo_ref, ho_ref = rest
        ho_ref[...] = xn.astype(ho_ref.dtype)
    else:
        (xo_ref,) = rest
    xo_ref[...] = xn


def _resnorm(x, f, gpost, gnext=None, mode="none", tm=ROW_TILE):
    s, d = x.shape
    row = pl.BlockSpec((tm, d), lambda i: (i, 0))
    vec = pl.BlockSpec((1, d), lambda i: (0, 0))
    in_specs = [row, row, vec]
    args = [x, f, gpost.reshape(1, d)]
    if mode == "norm":
        in_specs.append(vec)
        args.append(gnext.reshape(1, d))
    out_shape = [jax.ShapeDtypeStruct((s, d), F32)]
    out_specs = [row]
    if mode != "none":
        out_shape.append(jax.ShapeDtypeStruct((s, d), BF16))
        out_specs.append(row)
    out = pl.pallas_call(
        functools.partial(_resnorm_kernel, mode=mode),
        grid=(s // tm,),
        in_specs=in_specs,
        out_specs=out_specs,
        out_shape=out_shape,
        compiler_params=_params(("arbitrary",), 2 * tm * d * 14),
        name="resnorm_" + mode,
    )(*args)
    return out if mode != "none" else (out[0], None)


def _proj_kernel(a_ref, w_ref, *rest, epilogue):
    o_ref = rest[-1] if epilogue != "qk" else rest[-2]
    w = w_ref[...]
    if epilogue == "qk":
        g_ref, cos_ref, sin_ref, _, acc_ref = rest
        lane = lax.broadcasted_iota(jnp.int32, (1, HEAD_DIM), 1)
        first_half = (lane & ROT_HALF) == 0
        gain = g_ref[...]
    for mb in range(a_ref.shape[0] // PROJ_MATMUL_ROWS):
        mrows = slice(mb * PROJ_MATMUL_ROWS, (mb + 1) * PROJ_MATMUL_ROWS)
        acc = jnp.dot(a_ref[mrows, :], w, preferred_element_type=F32)
        if epilogue == "none":
            o_ref[mrows, :] = acc.astype(o_ref.dtype)
        elif epilogue == "transpose":
            o_ref[:, mrows] = acc.T.astype(o_ref.dtype)
        elif epilogue == "sigmoid":
            o_ref[mrows, :] = jax.nn.sigmoid(acc).astype(o_ref.dtype)
        elif epilogue == "ple":
            p_ref, wp_ref, _ = rest
            e = jnp.dot(p_ref[mrows, :], wp_ref[...], preferred_element_type=F32)
            o_ref[mrows, :] = (jax.nn.sigmoid(acc) * e).astype(o_ref.dtype)
        elif epilogue == "qk":
            acc_ref[mrows, :] = acc
            for r in range(PROJ_MATMUL_ROWS // QK_EPILOGUE_ROWS):
                start = mb * PROJ_MATMUL_ROWS + r * QK_EPILOGUE_ROWS
                rows = slice(start, start + QK_EPILOGUE_ROWS)
                cos = cos_ref[rows, :]
                sin = sin_ref[rows, :]
                for h in range(acc_ref.shape[1] // HEAD_DIM):
                    sl = slice(h * HEAD_DIM, (h + 1) * HEAD_DIM)
                    y = _rms(acc_ref[rows, sl], gain)
                    partner = jnp.where(first_half,
                                        pltpu.roll(y, HEAD_DIM - ROT_HALF, 1),
                                        pltpu.roll(y, ROT_HALF, 1))
                    o_ref[rows, sl] = (y * cos + partner * sin).astype(o_ref.dtype)
        else:
            raise ValueError(epilogue)


def _proj(a, w, col0, ncols, out_dtype, epilogue="none", extra=(), tm=PROJ_TILE_M, tn=PROJ_TILE_N):
    m, k = a.shape
    tn = min(tn, ncols)
    assert m % tm == 0 and ncols % tn == 0 and col0 % tn == 0 and w.shape[0] == k
    cb = col0 // tn
    in_specs = [pl.BlockSpec((tm, k), lambda i, j: (i, 0)),
                pl.BlockSpec((k, tn), lambda i, j: (0, cb + j))]
    args = [a, w]
    scratch = []
    extra_bytes = 0
    if epilogue == "qk":
        gain, cos, sin = extra
        in_specs += [pl.BlockSpec((1, HEAD_DIM), lambda i, j: (0, 0)),
                     pl.BlockSpec((tm, HEAD_DIM), lambda i, j: (i, 0)),
                     pl.BlockSpec((tm, HEAD_DIM), lambda i, j: (i, 0))]
        args += [gain.reshape(1, HEAD_DIM), cos, sin]
        scratch = [pltpu.VMEM((tm, tn), F32)]
        extra_bytes = tm * tn * 4 + 4 * tm * HEAD_DIM * 4
    elif epilogue == "ple":
        p, wp = extra
        kp = p.shape[1]
        in_specs += [pl.BlockSpec((tm, kp), lambda i, j: (i, 0)),
                     pl.BlockSpec((kp, tn), lambda i, j: (0, j))]
        args += [p, wp]
        extra_bytes = 2 * (tm * kp + kp * tn) * 2
    out_bytes = jnp.dtype(out_dtype).itemsize
    block_bytes = 2 * (tm * k * 2 + k * tn * 2 + tm * tn * out_bytes) + extra_bytes
    if epilogue == "transpose":
        out_spec = pl.BlockSpec((tn, tm), lambda i, j: (j, i))
        out_shape = jax.ShapeDtypeStruct((ncols, m), out_dtype)
    else:
        out_spec = pl.BlockSpec((tm, tn), lambda i, j: (i, j))
        out_shape = jax.ShapeDtypeStruct((m, ncols), out_dtype)
    return pl.pallas_call(
        functools.partial(_proj_kernel, epilogue=epilogue),
        grid=(m // tm, ncols // tn),
        in_specs=in_specs,
        out_specs=out_spec,
        out_shape=out_shape,
        scratch_shapes=scratch,
        compiler_params=_params(("arbitrary", "arbitrary"), block_bytes),
        name="proj_" + epilogue,
    )(*args)


def _flash_kernel(q_ref, k_ref, vt_ref, *rest, tk, n_cast):
    cast_in = rest[:n_cast]
    o_ref = rest[n_cast]
    cast_out = rest[n_cast + 1:2 * n_cast + 1]
    st_ref, pt_ref, acc_ref = rest[2 * n_cast + 1:]
    for src, dst in zip(cast_in, cast_out):
        dst[...] = src[...].astype(dst.dtype)
    tq = q_ref.shape[0]
    nchunks = k_ref.shape[0] // tk
    qs = [q_ref[:, h * HEAD_DIM:(h + 1) * HEAD_DIM] for h in range(Q_GROUP)]

    def k_chunk(c):
        return k_ref[pl.ds(pl.multiple_of(c * tk, tk), tk), :]

    def score(kc, h):
        return lax.dot_general(kc, qs[h], (((1,), (1,)), ((), ())), preferred_element_type=F32)

    def exact_softmax(c, ms, slot):
        kc = k_chunk(c)
        out = []
        for h in range(Q_GROUP):
            st = score(kc, h)
            st_ref[h] = st
            m_new = jnp.maximum(ms[h], jnp.max(st, axis=0, keepdims=True))
            alpha = jnp.exp2(ms[h] - m_new)
            for r in range(tk // SOFTMAX_ROWS):
                rows = slice(r * SOFTMAX_ROWS, (r + 1) * SOFTMAX_ROWS)
                pt_ref[slot, h, rows, :] = jnp.exp2(st_ref[h, rows, :] - m_new).astype(BF16)
            out.append((m_new, alpha))
        return tuple(zip(*out))

    def lagged_softmax(c, refs, slot):
        kc = k_chunk(c)
        cms = []
        for h in range(Q_GROUP):
            st = score(kc, h)
            pt_ref[slot, h] = jnp.exp2(st - refs[h]).astype(BF16)
            cms.append(jnp.max(st, axis=0, keepdims=True))
        return cms

    def advance(refs, excess, cms):
        new_refs = tuple(jnp.maximum(refs[h], cms[h]) for h in range(Q_GROUP))
        alphas = tuple(jnp.exp2(refs[h] - new_refs[h]) for h in range(Q_GROUP))
        excess = tuple(jnp.maximum(excess[h], cms[h] - refs[h]) for h in range(Q_GROUP))
        return new_refs, excess, alphas

    ones_rows = (lax.broadcasted_iota(jnp.int32, (BF16_SUBLANES, tk), 0) == 0).astype(BF16)

    def value_update(c, alphas, slot):
        vc = vt_ref[:, pl.ds(pl.multiple_of(c * tk, tk), tk)]
        vc = jnp.concatenate([vc, ones_rows], axis=0)
        for h in range(Q_GROUP):
            acc_ref[h] = alphas[h] * acc_ref[h] + jnp.dot(vc, pt_ref[slot, h],
                                                         preferred_element_type=F32)

    def finalize():
        for h in range(Q_GROUP):
            out_t = acc_ref[h, 0:HEAD_DIM, :] / acc_ref[h, HEAD_DIM:HEAD_DIM + 1, :]
            o_ref[:, h * HEAD_DIM:(h + 1) * HEAD_DIM] = out_t.T.astype(o_ref.dtype)

    def stage(c, k, carry):
        refs, excess, a_c2, a_c1, a_c = carry
        cms = lagged_softmax(c, refs, k % P_SLOTS)
        value_update(c - 2, a_c2, (k - 2) % P_SLOTS)
        refs, excess, a_next = advance(refs, excess, cms)
        return refs, excess, a_c1, a_c, a_next

    neg_inf = tuple(jnp.full((1, tq), -jnp.inf, F32) for _ in range(Q_GROUP))
    acc_ref[...] = jnp.zeros(acc_ref.shape, F32)
    refs, a0 = exact_softmax(0, neg_inf, 0)
    a1 = tuple(jnp.ones((1, tq), F32) for _ in range(Q_GROUP))
    cms = lagged_softmax(1, refs, 1)
    refs, excess, a2 = advance(refs, neg_inf, cms)
    carry = (refs, excess, a0, a1, a2)

    ntrips = (nchunks - 2) // P_SLOTS

    def body(t, carry):
        for k in range(2, 2 + P_SLOTS):
            carry = stage(P_SLOTS * t + k, k, carry)
        return carry

    carry = lax.fori_loop(0, ntrips, body, carry)
    for c in range(2 + P_SLOTS * ntrips, nchunks):
        carry = stage(c, c, carry)
    _, excess, a_c2, a_c1, _ = carry
    value_update(nchunks - 2, a_c2, (nchunks - 2) % P_SLOTS)
    value_update(nchunks - 1, a_c1, (nchunks - 1) % P_SLOTS)
    finalize()

    worst = excess[0]
    for h in range(1, Q_GROUP):
        worst = jnp.maximum(worst, excess[h])

    @pl.when(jnp.max(worst) > MAX_LAGGED_EXPONENT)
    def _():
        acc_ref[...] = jnp.zeros(acc_ref.shape, F32)

        def exact_body(c, ms):
            ms, alphas = exact_softmax(c, ms, 0)
            value_update(c, alphas, 0)
            return ms

        lax.fori_loop(0, nchunks, exact_body, neg_inf)
        finalize()


def _cast_block(shape, nsteps):
    rows, cols = shape
    ncol = 1
    while ncol <= nsteps:
        nrow = nsteps // ncol
        if (rows % (nrow * BF16_SUBLANES) == 0 and cols % (ncol * V7X_LANES) == 0
                and nrow * ncol == nsteps):
            return (rows // nrow, cols // ncol), ncol
        ncol *= 2
    raise ValueError(f"cannot tile {shape} over {nsteps} steps")


def _attention(q, k, vt, cast=(), tq=ATTN_TILE_Q, tk=ATTN_TILE_KV):
    s = q.shape[0]
    gw = Q_GROUP * HEAD_DIM
    nq = s // tq
    nsteps = N_KV_HEADS * nq
    block_bytes = (2 * (tq * gw * 2 + 2 * s * HEAD_DIM * 2 + tq * gw * 2)
                   + Q_GROUP * (tk * tq * (4 + 2 * P_SLOTS) + HEAD_DIM * tq * 4))
    cast_in_specs, cast_out_specs = [], []
    for w, layer in cast:
        blk, ncol = _cast_block(w.shape[1:], nsteps)
        cast_in_specs.append(pl.BlockSpec(
            (None,) + blk,
            lambda g, i, ncol=ncol, layer=layer: (layer, (g * nq + i) // ncol, (g * nq + i) % ncol)))
        cast_out_specs.append(pl.BlockSpec(
            blk, lambda g, i, ncol=ncol: ((g * nq + i) // ncol, (g * nq + i) % ncol)))
        block_bytes += 2 * blk[0] * blk[1] * (4 + 2)
    out = pl.pallas_call(
        functools.partial(_flash_kernel, tk=tk, n_cast=len(cast)),
        grid=(N_KV_HEADS, nq),
        in_specs=[pl.BlockSpec((tq, gw), lambda g, i: (i, g)),
                  pl.BlockSpec((s, HEAD_DIM), lambda g, i: (0, g)),
                  pl.BlockSpec((HEAD_DIM, s), lambda g, i: (g, 0))] + cast_in_specs,
        out_specs=[pl.BlockSpec((tq, gw), lambda g, i: (i, g))] + cast_out_specs,
        out_shape=[jax.ShapeDtypeStruct((s, ATTN_WIDTH), BF16)]
                  + [jax.ShapeDtypeStruct(w.shape[1:], BF16) for w, _ in cast],
        scratch_shapes=[pltpu.VMEM((Q_GROUP, tk, tq), F32),
                        pltpu.VMEM((P_SLOTS, Q_GROUP, tk, tq), BF16),
                        pltpu.VMEM((Q_GROUP, HEAD_DIM + BF16_SUBLANES, tq), F32)],
        compiler_params=_params(("arbitrary", "arbitrary"), block_bytes),
        name="flash_attention",
    )(q, k, vt, *[w for w, _ in cast])
    return out[0], out[1:]


def _pool_merge_kernel(up_ref, uc_ref, un_ref, wp_ref, ps_ref, gates_ref, attn_ref, o_ref, ext_ref,
                       *, seq):
    i = pl.program_id(0)
    tm = uc_ref.shape[0]
    last = pl.num_programs(0) - 1
    ext_ref[0:POOL_HALO, :] = jnp.where(i > 0, up_ref[...], 0.0)
    ext_ref[POOL_HALO:POOL_HALO + tm, :] = uc_ref[...]
    ext_ref[POOL_HALO + tm:, :] = jnp.where(i < last, un_ref[...], 0.0)
    t = i * tm + lax.broadcasted_iota(jnp.int32, (tm, 1), 0)
    for g in range(N_POOL_GROUPS):
        half = POOL_WINDOWS[g] // 2
        cols = slice(g * POOL_GROUP_IN, (g + 1) * POOL_GROUP_IN)
        wsum = ext_ref[POOL_HALO - half:POOL_HALO - half + tm, cols]
        for d in range(-half + 1, half):
            wsum = wsum + ext_ref[POOL_HALO + d:POOL_HALO + d + tm, cols]
        count = (jnp.minimum(t + half, seq) - jnp.maximum(t - half, 0)).astype(F32)
        delta = wsum / count - uc_ref[:, cols]
        y = jnp.dot(delta.astype(BF16), wp_ref[g], preferred_element_type=F32)
        oc = slice(g * POOL_GROUP_OUT, (g + 1) * POOL_GROUP_OUT)
        pool = y * ps_ref[:, oc]
        ga = gates_ref[:, oc].astype(F32)
        gp = gates_ref[:, D_MODEL + g * POOL_GROUP_OUT:D_MODEL + (g + 1) * POOL_GROUP_OUT].astype(F32)
        o_ref[:, oc] = (ga * attn_ref[:, oc].astype(F32) + gp * pool).astype(o_ref.dtype)


def _pool_merge(u, w_pool, pool_scale, gates, attn, tm=ROW_TILE):
    s = u.shape[0]
    assert POOL_HALO % V7X_SUBLANES == 0 and tm % POOL_HALO == 0
    hb = tm // POOL_HALO
    nhb = s // POOL_HALO
    block_bytes = (2 * (tm * POOL_WIDTH * 4 + 2 * POOL_HALO * POOL_WIDTH * 4
                        + tm * 2 * D_MODEL * gates.dtype.itemsize
                        + tm * D_MODEL * attn.dtype.itemsize + tm * D_MODEL * 2 + w_pool.size * 2)
                   + (tm + 2 * POOL_HALO) * POOL_WIDTH * 4)
    return pl.pallas_call(
        functools.partial(_pool_merge_kernel, seq=s),
        grid=(s // tm,),
        in_specs=[pl.BlockSpec((POOL_HALO, POOL_WIDTH), lambda i: (jnp.maximum(i * hb - 1, 0), 0)),
                  pl.BlockSpec((tm, POOL_WIDTH), lambda i: (i, 0)),
                  pl.BlockSpec((POOL_HALO, POOL_WIDTH), lambda i: (jnp.minimum((i + 1) * hb, nhb - 1), 0)),
                  pl.BlockSpec(w_pool.shape, lambda i: (0, 0, 0)),
                  pl.BlockSpec((1, D_MODEL), lambda i: (0, 0)),
                  pl.BlockSpec((tm, 2 * D_MODEL), lambda i: (i, 0)),
                  pl.BlockSpec((tm, D_MODEL), lambda i: (i, 0))],
        out_specs=pl.BlockSpec((tm, D_MODEL), lambda i: (i, 0)),
        out_shape=jax.ShapeDtypeStruct((s, D_MODEL), BF16),
        scratch_shapes=[pltpu.VMEM((tm + 2 * POOL_HALO, POOL_WIDTH), F32)],
        compiler_params=_params(("arbitrary",), block_bytes),
        name="pool_merge",
    )(u, u, u, w_pool, pool_scale.reshape(1, D_MODEL), gates, attn)


def _ffn_kernel(h_ref, wg_ref, wu_ref, wd_ref, o_ref):
    @pl.when(pl.program_id(1) == 0)
    def _():
        o_ref[...] = jnp.zeros(o_ref.shape, F32)

    h = h_ref[...]
    g = jnp.dot(h, wg_ref[...], preferred_element_type=F32)
    u = jnp.dot(h, wu_ref[...], preferred_element_type=F32)
    a = (g * jax.nn.sigmoid(g) * u).astype(BF16)
    o_ref[...] += jnp.dot(a, wd_ref[...], preferred_element_type=F32)


def _ffn(h, wg, wu, wd, tm=FFN_TILE_M, tf=FFN_TILE_FF):
    s, d = h.shape
    dff = wg.shape[1]
    assert dff % tf == 0
    block_bytes = 2 * (tm * d * 2 + 3 * d * tf * 2 + tm * d * 4)
    return pl.pallas_call(
        _ffn_kernel,
        grid=(s // tm, dff // tf),
        in_specs=[pl.BlockSpec((tm, d), lambda i, f: (i, 0)),
                  pl.BlockSpec((d, tf), lambda i, f: (0, f)),
                  pl.BlockSpec((d, tf), lambda i, f: (0, f)),
                  pl.BlockSpec((tf, d), lambda i, f: (f, 0))],
        out_specs=pl.BlockSpec((tm, d), lambda i, f: (i, 0)),
        out_shape=jax.ShapeDtypeStruct((s, d), F32),
        compiler_params=_params(("arbitrary", "arbitrary"), block_bytes),
        name="swiglu_ffn",
    )(h, wg, wu, wd)


def _rope_tables(seq):
    rows = seq // GRID_W
    row = jnp.broadcast_to(jnp.arange(rows)[:, None], (rows, GRID_W)).reshape(seq)
    col = jnp.broadcast_to(jnp.arange(GRID_W)[None, :], (rows, GRID_W)).reshape(seq)
    inv_freq = ROPE_THETA ** (-jnp.arange(0, AXIS_ROT_DIM, 2, dtype=F32) / AXIS_ROT_DIM)
    pos = jnp.stack([row, col], axis=-1).astype(F32)
    ang = pos[:, :, None] * inv_freq[None, None, :]
    cos, sin = jnp.cos(ang), jnp.sin(ang)
    cos_t = jnp.concatenate([cos, cos], axis=-1).reshape(seq, HEAD_DIM)
    sin_t = jnp.concatenate([-sin, sin], axis=-1).reshape(seq, HEAD_DIM)
    return cos_t, sin_t


def kernel(x, p, w_in, q_norm, k_norm, w_pool, pool_scale, w_out, norm_mix_pre, norm_mix_post,
           norm_ffn_pre, norm_ffn_post, w_ffn_gate, w_ffn_up, w_ffn_down, w_ple_in, w_ple_gate,
           norm_ple):
    b, s, d = x.shape
    assert b == 1 and d == D_MODEL and s % GRID_W == 0
    cos_t, sin_t = _rope_tables(s)
    xs = x.reshape(s, d)
    h = _norm_cast(xs, norm_mix_pre[0])
    w_in_b = w_in[0].astype(BF16)
    for i in range(DEPTH):
        q = _proj(h, w_in_b, COL_Q, ATTN_WIDTH, BF16, "qk",
                  (q_norm[i] * (HEAD_DIM ** -0.5 * LOG2_E), cos_t, sin_t))
        k = _proj(h, w_in_b, COL_K, KV_WIDTH, BF16, "qk", (k_norm[i], cos_t, sin_t))
        vt = _proj(h, w_in_b, COL_V, KV_WIDTH, BF16, "transpose")
        u = _proj(h, w_in_b, COL_U, POOL_WIDTH, F32)
        gates = _proj(h, w_in_b, COL_GATES, 2 * D_MODEL, BF16, "sigmoid")
        later = [(w_out, i), (w_ffn_gate, i), (w_ffn_up, i), (w_ffn_down, i), (w_ple_gate, i)]
        if i + 1 < DEPTH:
            later.append((w_in, i + 1))
        attn, later_b = _attention(q, k, vt, later)
        w_out_b, w_gate_b, w_up_b, w_down_b, w_ple_gate_b = later_b[:5]
        if i + 1 < DEPTH:
            w_in_b = later_b[5]
        merged = _pool_merge(u, w_pool[i].astype(BF16), pool_scale[i], gates, attn)
        o = _proj(merged, w_out_b, 0, D_MODEL, F32)
        xs, h = _resnorm(xs, o, norm_mix_post[i], norm_ffn_pre[i], "norm")
        f = _ffn(h, w_gate_b, w_up_b, w_down_b)
        xs, xb = _resnorm(xs, f, norm_ffn_post[i], None, "cast")
        ge = _proj(xb, w_ple_gate_b, 0, D_MODEL, F32, "ple",
                   (p[i].reshape(s, D_PLE).astype(BF16), w_ple_in[i].astype(BF16)))
        if i + 1 < DEPTH:
            xs, h = _resnorm(xs, ge, norm_ple[i], norm_mix_pre[i + 1], "norm")
        else:
            xs, _ = _resnorm(xs, ge, norm_ple[i], None, "none")
    return xs.reshape(b, s, d)
```

```python
import functools

import jax
import jax.numpy as jnp
from jax import lax
from jax.experimental import pallas as pl
from jax.experimental.pallas import tpu as pltpu

D_MODEL = 4096
DEPTH = 2
N_HEADS = 32
N_KV_HEADS = 8
HEAD_DIM = 128
Q_GROUP = N_HEADS // N_KV_HEADS
ATTN_WIDTH = N_HEADS * HEAD_DIM
KV_WIDTH = N_KV_HEADS * HEAD_DIM
AXIS_ROT_DIM = HEAD_DIM // 2
ROT_HALF = AXIS_ROT_DIM // 2
ROPE_THETA = 10000.0
GRID_W = 64
POOL_WINDOWS = (2, 4, 8, 16)
N_POOL_GROUPS = 4
POOL_WIDTH = 2048
POOL_GROUP_IN = POOL_WIDTH // N_POOL_GROUPS
POOL_GROUP_OUT = D_MODEL // N_POOL_GROUPS
D_PLE = 256
EPS = 1e-6
LOG2_E = 1.4426950408889634

COL_Q = 0
COL_K = COL_Q + ATTN_WIDTH
COL_V = COL_K + KV_WIDTH
COL_U = COL_V + KV_WIDTH
COL_GATES = COL_U + POOL_WIDTH

MIB = 1024 * 1024
V7X_VMEM_BYTES = 64 * MIB
V7X_VMEM_RESERVED_BYTES = 4 * MIB
V7X_SUBLANES = 8
V7X_LANES = 128
V7X_MXU_ROWS = 256
BF16_SUBLANES = 2 * V7X_SUBLANES
VMEM_TEMP_BYTES = 12 * MIB

ROW_TILE = 256
PROJ_TILE_M = 1024
PROJ_TILE_N = 1024
ATTN_TILE_Q = 512
ATTN_TILE_KV = 512
FFN_TILE_M = 512
FFN_TILE_FF = 256

POOL_HALO = max(POOL_WINDOWS) // 2
SOFTMAX_ROWS = 32
QK_EPILOGUE_ROWS = 128
PROJ_MATMUL_ROWS = V7X_MXU_ROWS
MAX_LAGGED_EXPONENT = 64.0
P_SLOTS = 3

F32 = jnp.float32
BF16 = jnp.bfloat16


def _params(semantics, block_bytes):
    limit = min(int(block_bytes) + VMEM_TEMP_BYTES, V7X_VMEM_BYTES - V7X_VMEM_RESERVED_BYTES)
    return pltpu.CompilerParams(dimension_semantics=semantics, vmem_limit_bytes=limit)


def _rms(x, gain):
    ms = jnp.mean(x * x, axis=-1, keepdims=True)
    return x * lax.rsqrt(ms + EPS) * gain


def _norm_cast_kernel(x_ref, g_ref, o_ref):
    o_ref[...] = _rms(x_ref[...], g_ref[...]).astype(o_ref.dtype)


def _norm_cast(x, gain, tm=ROW_TILE):
    s, d = x.shape
    return pl.pallas_call(
        _norm_cast_kernel,
        grid=(s // tm,),
        in_specs=[pl.BlockSpec((tm, d), lambda i: (i, 0)),
                  pl.BlockSpec((1, d), lambda i: (0, 0))],
        out_specs=pl.BlockSpec((tm, d), lambda i: (i, 0)),
        out_shape=jax.ShapeDtypeStruct((s, d), BF16),
        compiler_params=_params(("arbitrary",), 2 * tm * d * 6),
        name="norm_cast",
    )(x, gain.reshape(1, d))


def _resnorm_kernel(x_ref, f_ref, gpost_ref, *rest, mode):
    xn = x_ref[...] + _rms(f_ref[...], gpost_ref[...])
    if mode == "norm":
        gnext_ref, xo_ref, ho_ref = rest
        ho_ref[...] = _rms(xn, gnext_ref[...]).astype(ho_ref.dtype)
    elif mode == "cast":
        xo_ref, ho_ref = rest
        ho_ref[...] = xn.astype(ho_ref.dtype)
    else:
        (xo_ref,) = rest
    xo_ref[...] = xn


def _resnorm(x, f, gpost, gnext=None, mode="none", tm=ROW_TILE):
    s, d = x.shape
    row = pl.BlockSpec((tm, d), lambda i: (i, 0))
    vec = pl.BlockSpec((1, d), lambda i: (0, 0))
    in_specs = [row, row, vec]
    args = [x, f, gpost.reshape(1, d)]
    if mode == "norm":
        in_specs.append(vec)
        args.append(gnext.reshape(1, d))
    out_shape = [jax.ShapeDtypeStruct((s, d), F32)]
    out_specs = [row]
    if mode != "none":
        out_shape.append(jax.ShapeDtypeStruct((s, d), BF16))
        out_specs.append(row)
    out = pl.pallas_call(
        functools.partial(_resnorm_kernel, mode=mode),
        grid=(s // tm,),
        in_specs=in_specs,
        out_specs=out_specs,
        out_shape=out_shape,
        compiler_params=_params(("arbitrary",), 2 * tm * d * 14),
        name="resnorm_" + mode,
    )(*args)
    return out if mode != "none" else (out[0], None)


def _proj_kernel(a_ref, w_ref, *rest, epilogue):
    o_ref = rest[-1] if epilogue != "qk" else rest[-2]
    w = w_ref[...]
    if epilogue == "qk":
        g_ref, cos_ref, sin_ref, _, acc_ref = rest
        lane = lax.broadcasted_iota(jnp.int32, (1, HEAD_DIM), 1)
        first_half = (lane & ROT_HALF) == 0
        gain = g_ref[...]
    for mb in range(a_ref.shape[0] // PROJ_MATMUL_ROWS):
        mrows = slice(mb * PROJ_MATMUL_ROWS, (mb + 1) * PROJ_MATMUL_ROWS)
        acc = jnp.dot(a_ref[mrows, :], w, preferred_element_type=F32)
        if epilogue == "none":
            o_ref[mrows, :] = acc.astype(o_ref.dtype)
        elif epilogue == "transpose":
            o_ref[:, mrows] = acc.T.astype(o_ref.dtype)
        elif epilogue == "sigmoid":
            o_ref[mrows, :] = jax.nn.sigmoid(acc).astype(o_ref.dtype)
        elif epilogue == "ple":
            p_ref, wp_ref, _ = rest
            e = jnp.dot(p_ref[mrows, :], wp_ref[...], preferred_element_type=F32)
            o_ref[mrows, :] = (jax.nn.sigmoid(acc) * e).astype(o_ref.dtype)
        elif epilogue == "qk":
            acc_ref[mrows, :] = acc
            for r in range(PROJ_MATMUL_ROWS // QK_EPILOGUE_ROWS):
                start = mb * PROJ_MATMUL_ROWS + r * QK_EPILOGUE_ROWS
                rows = slice(start, start + QK_EPILOGUE_ROWS)
                cos = cos_ref[rows, :]
                sin = sin_ref[rows, :]
                for h in range(acc_ref.shape[1] // HEAD_DIM):
                    sl = slice(h * HEAD_DIM, (h + 1) * HEAD_DIM)
                    y = _rms(acc_ref[rows, sl], gain)
                    partner = jnp.where(first_half,
                                        pltpu.roll(y, HEAD_DIM - ROT_HALF, 1),
                                        pltpu.roll(y, ROT_HALF, 1))
                    o_ref[rows, sl] = (y * cos + partner * sin).astype(o_ref.dtype)
        else:
            raise ValueError(epilogue)


def _proj(a, w, col0, ncols, out_dtype, epilogue="none", extra=(), tm=PROJ_TILE_M, tn=PROJ_TILE_N):
    m, k = a.shape
    tn = min(tn, ncols)
    assert m % tm == 0 and ncols % tn == 0 and col0 % tn == 0 and w.shape[0] == k
    cb = col0 // tn
    in_specs = [pl.BlockSpec((tm, k), lambda i, j: (i, 0)),
                pl.BlockSpec((k, tn), lambda i, j: (0, cb + j))]
    args = [a, w]
    scratch = []
    extra_bytes = 0
    if epilogue == "qk":
        gain, cos, sin = extra
        in_specs += [pl.BlockSpec((1, HEAD_DIM), lambda i, j: (0, 0)),
                     pl.BlockSpec((tm, HEAD_DIM), lambda i, j: (i, 0)),
                     pl.BlockSpec((tm, HEAD_DIM), lambda i, j: (i, 0))]
        args += [gain.reshape(1, HEAD_DIM), cos, sin]
        scratch = [pltpu.VMEM((tm, tn), F32)]
        extra_bytes = tm * tn * 4 + 4 * tm * HEAD_DIM * 4
    elif epilogue == "ple":
        p, wp = extra
        kp = p.shape[1]
        in_specs += [pl.BlockSpec((tm, kp), lambda i, j: (i, 0)),
                     pl.BlockSpec((kp, tn), lambda i, j: (0, j))]
        args += [p, wp]
        extra_bytes = 2 * (tm * kp + kp * tn) * 2
    out_bytes = jnp.dtype(out_dtype).itemsize
    block_bytes = 2 * (tm * k * 2 + k * tn * 2 + tm * tn * out_bytes) + extra_bytes
    if epilogue == "transpose":
        out_spec = pl.BlockSpec((tn, tm), lambda i, j: (j, i))
        out_shape = jax.ShapeDtypeStruct((ncols, m), out_dtype)
    else:
        out_spec = pl.BlockSpec((tm, tn), lambda i, j: (i, j))
        out_shape = jax.ShapeDtypeStruct((m, ncols), out_dtype)
    return pl.pallas_call(
        functools.partial(_proj_kernel, epilogue=epilogue),
        grid=(m // tm, ncols // tn),
        in_specs=in_specs,
        out_specs=out_spec,
        out_shape=out_shape,
        scratch_shapes=scratch,
        compiler_params=_params(("arbitrary", "arbitrary"), block_bytes),
        name="proj_" + epilogue,
    )(*args)


def _flash_kernel(q_ref, k_ref, vt_ref, *rest, tk, n_cast):
    cast_in = rest[:n_cast]
    o_ref = rest[n_cast]
    cast_out = rest[n_cast + 1:2 * n_cast + 1]
    st_ref, pt_ref, acc_ref = rest[2 * n_cast + 1:]
    for src, dst in zip(cast_in, cast_out):
        dst[...] = src[...].astype(dst.dtype)
    tq = q_ref.shape[0]
    nchunks = k_ref.shape[0] // tk
    qs = [q_ref[:, h * HEAD_DIM:(h + 1) * HEAD_DIM] for h in range(Q_GROUP)]

    def k_chunk(c):
        return k_ref[pl.ds(pl.multiple_of(c * tk, tk), tk), :]

    def score(kc, h):
        return lax.dot_general(kc, qs[h], (((1,), (1,)), ((), ())), preferred_element_type=F32)

    def exact_softmax(c, ms, slot):
        kc = k_chunk(c)
        out = []
        for h in range(Q_GROUP):
            st = score(kc, h)
            st_ref[h] = st
            m_new = jnp.maximum(ms[h], jnp.max(st, axis=0, keepdims=True))
            alpha = jnp.exp2(ms[h] - m_new)
            for r in range(tk // SOFTMAX_ROWS):
                rows = slice(r * SOFTMAX_ROWS, (r + 1) * SOFTMAX_ROWS)
                pt_ref[slot, h, rows, :] = jnp.exp2(st_ref[h, rows, :] - m_new).astype(BF16)
            out.append((m_new, alpha))
        return tuple(zip(*out))

    def lagged_softmax(c, refs, slot):
        kc = k_chunk(c)
        cms = []
        for h in range(Q_GROUP):
            st = score(kc, h)
            pt_ref[slot, h] = jnp.exp2(st - refs[h]).astype(BF16)
            cms.append(jnp.max(st, axis=0, keepdims=True))
        return cms

    def advance(refs, excess, cms):
        new_refs = tuple(jnp.maximum(refs[h], cms[h]) for h in range(Q_GROUP))
        alphas = tuple(jnp.exp2(refs[h] - new_refs[h]) for h in range(Q_GROUP))
        excess = tuple(jnp.maximum(excess[h], cms[h] - refs[h]) for h in range(Q_GROUP))
        return new_refs, excess, alphas

    ones_rows = (lax.broadcasted_iota(jnp.int32, (BF16_SUBLANES, tk), 0) == 0).astype(BF16)

    def value_update(c, alphas, slot):
        vc = vt_ref[:, pl.ds(pl.multiple_of(c * tk, tk), tk)]
        vc = jnp.concatenate([vc, ones_rows], axis=0)
        for h in range(Q_GROUP):
            acc_ref[h] = alphas[h] * acc_ref[h] + jnp.dot(vc, pt_ref[slot, h],
                                                         preferred_element_type=F32)

    def finalize():
        for h in range(Q_GROUP):
            out_t = acc_ref[h, 0:HEAD_DIM, :] / acc_ref[h, HEAD_DIM:HEAD_DIM + 1, :]
            o_ref[:, h * HEAD_DIM:(h + 1) * HEAD_DIM] = out_t.T.astype(o_ref.dtype)

    def stage(c, k, carry):
        refs, excess, a_c2, a_c1, a_c = carry
        cms = lagged_softmax(c, refs, k % P_SLOTS)
        value_update(c - 2, a_c2, (k - 2) % P_SLOTS)
        refs, excess, a_next = advance(refs, excess, cms)
        return refs, excess, a_c1, a_c, a_next

    neg_inf = tuple(jnp.full((1, tq), -jnp.inf, F32) for _ in range(Q_GROUP))
    acc_ref[...] = jnp.zeros(acc_ref.shape, F32)
    refs, a0 = exact_softmax(0, neg_inf, 0)
    a1 = tuple(jnp.ones((1, tq), F32) for _ in range(Q_GROUP))
    cms = lagged_softmax(1, refs, 1)
    refs, excess, a2 = advance(refs, neg_inf, cms)
    carry = (refs, excess, a0, a1, a2)

    period = 2 * P_SLOTS
    ntrips = (nchunks - 2) // period

    def body(t, carry):
        for k in range(2, 2 + period):
            carry = stage(period * t + k, k, carry)
        return carry

    carry = lax.fori_loop(0, ntrips, body, carry)
    for c in range(2 + period * ntrips, nchunks):
        carry = stage(c, c, carry)
    _, excess, a_c2, a_c1, _ = carry
    value_update(nchunks - 2, a_c2, (nchunks - 2) % P_SLOTS)
    value_update(nchunks - 1, a_c1, (nchunks - 1) % P_SLOTS)
    finalize()

    worst = excess[0]
    for h in range(1, Q_GROUP):
        worst = jnp.maximum(worst, excess[h])

    @pl.when(jnp.max(worst) > MAX_LAGGED_EXPONENT)
    def _():
        acc_ref[...] = jnp.zeros(acc_ref.shape, F32)

        def exact_body(c, ms):
            ms, alphas = exact_softmax(c, ms, 0)
            value_update(c, alphas, 0)
            return ms

        lax.fori_loop(0, nchunks, exact_body, neg_inf)
        finalize()


def _cast_block(shape, nsteps):
    rows, cols = shape
    ncol = 1
    while ncol <= nsteps:
        nrow = nsteps // ncol
        if (rows % (nrow * BF16_SUBLANES) == 0 and cols % (ncol * V7X_LANES) == 0
                and nrow * ncol == nsteps):
            return (rows // nrow, cols // ncol), ncol
        ncol *= 2
    raise ValueError(f"cannot tile {shape} over {nsteps} steps")


def _attention(q, k, vt, cast=(), tq=ATTN_TILE_Q, tk=ATTN_TILE_KV):
    s = q.shape[0]
    gw = Q_GROUP * HEAD_DIM
    nq = s // tq
    nsteps = N_KV_HEADS * nq
    block_bytes = (2 * (tq * gw * 2 + 2 * s * HEAD_DIM * 2 + tq * gw * 2)
                   + Q_GROUP * (tk * tq * (4 + 2 * P_SLOTS) + HEAD_DIM * tq * 4))
    cast_in_specs, cast_out_specs = [], []
    for w, layer in cast:
        blk, ncol = _cast_block(w.shape[1:], nsteps)
        cast_in_specs.append(pl.BlockSpec(
            (None,) + blk,
            lambda g, i, ncol=ncol, layer=layer: (layer, (g * nq + i) // ncol, (g * nq + i) % ncol)))
        cast_out_specs.append(pl.BlockSpec(
            blk, lambda g, i, ncol=ncol: ((g * nq + i) // ncol, (g * nq + i) % ncol)))
        block_bytes += 2 * blk[0] * blk[1] * (4 + 2)
    out = pl.pallas_call(
        functools.partial(_flash_kernel, tk=tk, n_cast=len(cast)),
        grid=(N_KV_HEADS, nq),
        in_specs=[pl.BlockSpec((tq, gw), lambda g, i: (i, g)),
                  pl.BlockSpec((s, HEAD_DIM), lambda g, i: (0, g)),
                  pl.BlockSpec((HEAD_DIM, s), lambda g, i: (g, 0))] + cast_in_specs,
        out_specs=[pl.BlockSpec((tq, gw), lambda g, i: (i, g))] + cast_out_specs,
        out_shape=[jax.ShapeDtypeStruct((s, ATTN_WIDTH), BF16)]
                  + [jax.ShapeDtypeStruct(w.shape[1:], BF16) for w, _ in cast],
        scratch_shapes=[pltpu.VMEM((Q_GROUP, tk, tq), F32),
                        pltpu.VMEM((P_SLOTS, Q_GROUP, tk, tq), BF16),
                        pltpu.VMEM((Q_GROUP, HEAD_DIM + BF16_SUBLANES, tq), F32)],
        compiler_params=_params(("arbitrary", "arbitrary"), block_bytes),
        name="flash_attention",
    )(q, k, vt, *[w for w, _ in cast])
    return out[0], out[1:]


def _pool_merge_kernel(up_ref, uc_ref, un_ref, wp_ref, ps_ref, gates_ref, attn_ref, o_ref, ext_ref,
                       *, seq):
    i = pl.program_id(0)
    tm = uc_ref.shape[0]
    last = pl.num_programs(0) - 1
    ext_ref[0:POOL_HALO, :] = jnp.where(i > 0, up_ref[...], 0.0)
    ext_ref[POOL_HALO:POOL_HALO + tm, :] = uc_ref[...]
    ext_ref[POOL_HALO + tm:, :] = jnp.where(i < last, un_ref[...], 0.0)
    t = i * tm + lax.broadcasted_iota(jnp.int32, (tm, 1), 0)
    for g in range(N_POOL_GROUPS):
        half = POOL_WINDOWS[g] // 2
        cols = slice(g * POOL_GROUP_IN, (g + 1) * POOL_GROUP_IN)
        wsum = ext_ref[POOL_HALO - half:POOL_HALO - half + tm, cols]
        for d in range(-half + 1, half):
            wsum = wsum + ext_ref[POOL_HALO + d:POOL_HALO + d + tm, cols]
        count = (jnp.minimum(t + half, seq) - jnp.maximum(t - half, 0)).astype(F32)
        delta = wsum / count - uc_ref[:, cols]
        y = jnp.dot(delta.astype(BF16), wp_ref[g], preferred_element_type=F32)
        oc = slice(g * POOL_GROUP_OUT, (g + 1) * POOL_GROUP_OUT)
        pool = y * ps_ref[:, oc]
        ga = gates_ref[:, oc].astype(F32)
        gp = gates_ref[:, D_MODEL + g * POOL_GROUP_OUT:D_MODEL + (g + 1) * POOL_GROUP_OUT].astype(F32)
        o_ref[:, oc] = (ga * attn_ref[:, oc].astype(F32) + gp * pool).astype(o_ref.dtype)


def _pool_merge(u, w_pool, pool_scale, gates, attn, tm=ROW_TILE):
    s = u.shape[0]
    assert POOL_HALO % V7X_SUBLANES == 0 and tm % POOL_HALO == 0
    hb = tm // POOL_HALO
    nhb = s // POOL_HALO
    block_bytes = (2 * (tm * POOL_WIDTH * 4 + 2 * POOL_HALO * POOL_WIDTH * 4
                        + tm * 2 * D_MODEL * gates.dtype.itemsize
                        + tm * D_MODEL * attn.dtype.itemsize + tm * D_MODEL * 2 + w_pool.size * 2)
                   + (tm + 2 * POOL_HALO) * POOL_WIDTH * 4)
    return pl.pallas_call(
        functools.partial(_pool_merge_kernel, seq=s),
        grid=(s // tm,),
        in_specs=[pl.BlockSpec((POOL_HALO, POOL_WIDTH), lambda i: (jnp.maximum(i * hb - 1, 0), 0)),
                  pl.BlockSpec((tm, POOL_WIDTH), lambda i: (i, 0)),
                  pl.BlockSpec((POOL_HALO, POOL_WIDTH), lambda i: (jnp.minimum((i + 1) * hb, nhb - 1), 0)),
                  pl.BlockSpec(w_pool.shape, lambda i: (0, 0, 0)),
                  pl.BlockSpec((1, D_MODEL), lambda i: (0, 0)),
                  pl.BlockSpec((tm, 2 * D_MODEL), lambda i: (i, 0)),
                  pl.BlockSpec((tm, D_MODEL), lambda i: (i, 0))],
        out_specs=pl.BlockSpec((tm, D_MODEL), lambda i: (i, 0)),
        out_shape=jax.ShapeDtypeStruct((s, D_MODEL), BF16),
        scratch_shapes=[pltpu.VMEM((tm + 2 * POOL_HALO, POOL_WIDTH), F32)],
        compiler_params=_params(("arbitrary",), block_bytes),
        name="pool_merge",
    )(u, u, u, w_pool, pool_scale.reshape(1, D_MODEL), gates, attn)


def _ffn_kernel(h_ref, wg_ref, wu_ref, wd_ref, o_ref):
    @pl.when(pl.program_id(1) == 0)
    def _():
        o_ref[...] = jnp.zeros(o_ref.shape, F32)

    h = h_ref[...]
    g = jnp.dot(h, wg_ref[...], preferred_element_type=F32)
    u = jnp.dot(h, wu_ref[...], preferred_element_type=F32)
    a = (g * jax.nn.sigmoid(g) * u).astype(BF16)
    o_ref[...] += jnp.dot(a, wd_ref[...], preferred_element_type=F32)


def _ffn(h, wg, wu, wd, tm=FFN_TILE_M, tf=FFN_TILE_FF):
    s, d = h.shape
    dff = wg.shape[1]
    assert dff % tf == 0
    block_bytes = 2 * (tm * d * 2 + 3 * d * tf * 2 + tm * d * 4)
    return pl.pallas_call(
        _ffn_kernel,
        grid=(s // tm, dff // tf),
        in_specs=[pl.BlockSpec((tm, d), lambda i, f: (i, 0)),
                  pl.BlockSpec((d, tf), lambda i, f: (0, f)),
                  pl.BlockSpec((d, tf), lambda i, f: (0, f)),
                  pl.BlockSpec((tf, d), lambda i, f: (f, 0))],
        out_specs=pl.BlockSpec((tm, d), lambda i, f: (i, 0)),
        out_shape=jax.ShapeDtypeStruct((s, d), F32),
        compiler_params=_params(("arbitrary", "arbitrary"), block_bytes),
        name="swiglu_ffn",
    )(h, wg, wu, wd)


def _rope_tables(seq):
    rows = seq // GRID_W
    row = jnp.broadcast_to(jnp.arange(rows)[:, None], (rows, GRID_W)).reshape(seq)
    col = jnp.broadcast_to(jnp.arange(GRID_W)[None, :], (rows, GRID_W)).reshape(seq)
    inv_freq = ROPE_THETA ** (-jnp.arange(0, AXIS_ROT_DIM, 2, dtype=F32) / AXIS_ROT_DIM)
    pos = jnp.stack([row, col], axis=-1).astype(F32)
    ang = pos[:, :, None] * inv_freq[None, None, :]
    cos, sin = jnp.cos(ang), jnp.sin(ang)
    cos_t = jnp.concatenate([cos, cos], axis=-1).reshape(seq, HEAD_DIM)
    sin_t = jnp.concatenate([-sin, sin], axis=-1).reshape(seq, HEAD_DIM)
    return cos_t, sin_t


def kernel(x, p, w_in, q_norm, k_norm, w_pool, pool_scale, w_out, norm_mix_pre, norm_mix_post,
           norm_ffn_pre, norm_ffn_post, w_ffn_gate, w_ffn_up, w_ffn_down, w_ple_in, w_ple_gate,
           norm_ple):
    b, s, d = x.shape
    assert b == 1 and d == D_MODEL and s % GRID_W == 0
    cos_t, sin_t = _rope_tables(s)
    xs = x.reshape(s, d)
    h = _norm_cast(xs, norm_mix_pre[0])
    w_in_b = w_in[0].astype(BF16)
    for i in range(DEPTH):
        q = _proj(h, w_in_b, COL_Q, ATTN_WIDTH, BF16, "qk",
                  (q_norm[i] * (HEAD_DIM ** -0.5 * LOG2_E), cos_t, sin_t))
        k = _proj(h, w_in_b, COL_K, KV_WIDTH, BF16, "qk", (k_norm[i], cos_t, sin_t))
        vt = _proj(h, w_in_b, COL_V, KV_WIDTH, BF16, "transpose")
        u = _proj(h, w_in_b, COL_U, POOL_WIDTH, F32)
        gates = _proj(h, w_in_b, COL_GATES, 2 * D_MODEL, BF16, "sigmoid")
        later = [(w_out, i), (w_ffn_gate, i), (w_ffn_up, i), (w_ffn_down, i), (w_ple_gate, i)]
        if i + 1 < DEPTH:
            later.append((w_in, i + 1))
        attn, later_b = _attention(q, k, vt, later)
        w_out_b, w_gate_b, w_up_b, w_down_b, w_ple_gate_b = later_b[:5]
        if i + 1 < DEPTH:
            w_in_b = later_b[5]
        merged = _pool_merge(u, w_pool[i].astype(BF16), pool_scale[i], gates, attn)
        o = _proj(merged, w_out_b, 0, D_MODEL, F32)
        xs, h = _resnorm(xs, o, norm_mix_post[i], norm_ffn_pre[i], "norm")
        f = _ffn(h, w_gate_b, w_up_b, w_down_b)
        xs, xb = _resnorm(xs, f, norm_ffn_post[i], None, "cast")
        ge = _proj(xb, w_ple_gate_b, 0, D_MODEL, F32, "ple",
                   (p[i].reshape(s, D_PLE).astype(BF16), w_ple_in[i].astype(BF16)))
        if i + 1 < DEPTH:
            xs, h = _resnorm(xs, ge, norm_ple[i], norm_mix_pre[i + 1], "norm")
        else:
            xs, _ = _resnorm(xs, ge, norm_ple[i], None, "none")
    return xs.reshape(b, s, d)
```
